```python
import jax, jax.numpy as jnp
from jax import lax
import numpy as np

D_MODEL = 2048
BATCH = 2
SEQ = 4096
DEPTH = 1

CHUNK = 64
POOL_WINDOWS = (2, 4, 8, 16)
POOL_WIDTH = D_MODEL // 2
POOL_GROUP = POOL_WIDTH // len(POOL_WINDOWS)
RET_HEADS = 8
RET_QK_DIM = 128
RET_V_DIM = 256
RET_QK_WIDTH = RET_HEADS * RET_QK_DIM
RET_V_WIDTH = RET_HEADS * RET_V_DIM
D_FF = 5632
CONV_WIDTH = 3
ROPE_BASE = 10000.0
RMS_EPS = 1e-6
GN_EPS = 1e-5
IN_SPLITS = (POOL_WIDTH, RET_QK_WIDTH, RET_QK_WIDTH, RET_V_WIDTH, RET_V_WIDTH, D_MODEL, D_MODEL)
IN_WIDTH = POOL_WIDTH + 2 * RET_QK_WIDTH + 2 * RET_V_WIDTH + 2 * D_MODEL

kernel_name = "hybrid_pool_retention_convglu_block"


def rmsnorm(x, g):
    xf = x.astype(jnp.float32)
    y = xf * lax.rsqrt(jnp.mean(xf * xf, axis=-1, keepdims=True) + RMS_EPS)
    return (y * g.astype(jnp.float32)).astype(x.dtype)


def pool_branch(p, w_group, scale):
    B, T, _ = p.shape
    pf = p.astype(jnp.float32)
    csum = jnp.pad(jnp.cumsum(pf, axis=1), ((0, 0), (1, 0), (0, 0)))
    t = jnp.arange(T)
    means = []
    for gi, w in enumerate(POOL_WINDOWS):
        cg = csum[:, :, gi * POOL_GROUP:(gi + 1) * POOL_GROUP]
        upper = cg[:, 1:]
        lower = jnp.take(cg, jnp.maximum(t + 1 - w, 0), axis=1)
        cnt = jnp.minimum(t + 1, w).astype(jnp.float32)
        means.append((upper - lower) / cnt[None, :, None])
    pooled = jnp.concatenate(means, axis=-1) - pf
    pooled = pooled.reshape(B, T, len(POOL_WINDOWS), POOL_GROUP)
    mixed = jnp.einsum('btgc,gcd->btgd', pooled, w_group.astype(jnp.float32))
    mixed = mixed.reshape(B, T, POOL_WIDTH) * scale.astype(jnp.float32)
    return mixed.astype(p.dtype)


def rotary(x):
    T, d = x.shape[1], x.shape[-1]
    half = d // 2
    inv = ROPE_BASE ** (-jnp.arange(half, dtype=jnp.float32) / half)
    ang = jnp.arange(T, dtype=jnp.float32)[:, None] * inv[None, :]
    cos = jnp.cos(ang)[None, :, None, :]
    sin = jnp.sin(ang)[None, :, None, :]
    x1, x2 = x[..., :half], x[..., half:]
    return jnp.concatenate([x1 * cos - x2 * sin, x2 * cos + x1 * sin], axis=-1)


def retention(q, k, v):
    B, T, H, dk = q.shape
    dv = v.shape[-1]
    N = T // CHUNK
    log_g = jnp.log(1.0 - 2.0 ** (-5.0 - jnp.arange(H, dtype=jnp.float32)))
    j = jnp.arange(CHUNK, dtype=jnp.float32)
    intra_decay = jnp.exp(jnp.abs(j[:, None] - j[None, :])[None] * log_g[:, None, None])
    q_decay = jnp.exp((j + 1.0)[:, None] * log_g[None, :])
    k_decay = jnp.exp((CHUNK - 1.0 - j)[:, None] * log_g[None, :])
    chunk_decay = jnp.exp(CHUNK * log_g)

    qc = q.reshape(B, N, CHUNK, H, dk)
    kc = k.reshape(B, N, CHUNK, H, dk)
    vc = v.reshape(B, N, CHUNK, H, dv)

    scores = jnp.einsum('bnihd,bnjhd->bnhij', qc, kc) * intra_decay[None, None]
    intra = jnp.einsum('bnhij,bnjhe->bnihe', scores, vc)

    def step(state, inp):
        qn, kn, vn = inp
        cross = jnp.einsum('bihd,bhde->bihe', qn * q_decay[None, :, :, None], state)
        state = state * chunk_decay[None, :, None, None] + jnp.einsum(
            'bjhd,bjhe->bhde', kn * k_decay[None, :, :, None], vn)
        return state, cross

    s0 = jnp.zeros((B, H, dk, dv), jnp.float32)
    _, cross = lax.scan(step, s0, (qc.swapaxes(0, 1), kc.swapaxes(0, 1), vc.swapaxes(0, 1)))
    out = intra + cross.swapaxes(0, 1)
    return out.reshape(B, T, H, dv)


def retention_branch(q, k, v, g, gn_gain):
    B, T, _ = q.shape
    qf = rotary(q.astype(jnp.float32).reshape(B, T, RET_HEADS, RET_QK_DIM))
    kf = rotary(k.astype(jnp.float32).reshape(B, T, RET_HEADS, RET_QK_DIM)) * (RET_QK_DIM ** -0.5)
    vf = v.astype(jnp.float32).reshape(B, T, RET_HEADS, RET_V_DIM)
    o = retention(qf, kf, vf)
    mu = jnp.mean(o, axis=-1, keepdims=True)
    var = jnp.mean(jnp.square(o - mu), axis=-1, keepdims=True)
    o = (o - mu) * lax.rsqrt(var + GN_EPS)
    o = o.reshape(B, T, RET_V_WIDTH) * gn_gain.astype(jnp.float32)
    return (jax.nn.silu(g.astype(jnp.float32)) * o).astype(q.dtype)


def conv_glu(h, w_up, conv_w, conv_b, w_down):
    ug = h @ w_up
    gate, up = ug[..., :D_FF], ug[..., D_FF:]
    gate = lax.conv_general_dilated(
        gate, conv_w[:, None, :].astype(gate.dtype), window_strides=(1,),
        padding=((CONV_WIDTH - 1, 0),), dimension_numbers=('NWC', 'WIO', 'NWC'),
        feature_group_count=D_FF) + conv_b
    return (jax.nn.silu(gate) * up) @ w_down


def setup_inputs(seed: int = 0) -> dict:
    key = jax.random.key(seed)
    ks = jax.random.split(key, 16)
    f32 = jnp.float32

    def w(k, shape, fan_in):
        return jax.random.normal(k, shape, f32) * (fan_in ** -0.5)

    def gain(k, shape):
        return 1.0 + 0.02 * jax.random.normal(k, shape, f32)

    return {
        "x": jax.random.normal(ks[0], (BATCH, SEQ, D_MODEL), f32),
        "norm_mix": gain(ks[1], (DEPTH, D_MODEL)),
        "w_in": w(ks[2], (DEPTH, D_MODEL, IN_WIDTH), D_MODEL),
        "w_pool_group": w(ks[3], (DEPTH, len(POOL_WINDOWS), POOL_GROUP, POOL_GROUP), POOL_GROUP),
        "pool_scale": gain(ks[4], (DEPTH, POOL_WIDTH)),
        "w_pool_proj": w(ks[5], (DEPTH, POOL_WIDTH, D_MODEL), POOL_WIDTH),
        "ret_gn_gain": gain(ks[6], (DEPTH, RET_V_WIDTH)),
        "w_ret_proj": w(ks[7], (DEPTH, RET_V_WIDTH, D_MODEL), RET_V_WIDTH),
        "w_out": w(ks[8], (DEPTH, D_MODEL, D_MODEL), D_MODEL),
        "norm_ffn": gain(ks[9], (DEPTH, D_MODEL)),
        "w_up": w(ks[10], (DEPTH, D_MODEL, 2 * D_FF), D_MODEL),
        "conv_w": w(ks[11], (DEPTH, CONV_WIDTH, D_FF), CONV_WIDTH),
        "conv_b": 0.02 * jax.random.normal(ks[12], (DEPTH, D_FF), f32),
        "w_down": w(ks[13], (DEPTH, D_FF, D_MODEL), D_FF),
        "norm_final": gain(ks[14], (D_MODEL,)),
    }


def reference(x, norm_mix, w_in, w_pool_group, pool_scale, w_pool_proj, ret_gn_gain,
              w_ret_proj, w_out, norm_ffn, w_up, conv_w, conv_b, w_down, norm_final):
    split_idx = np.cumsum(IN_SPLITS)[:-1].tolist()
    h = x
    for l in range(DEPTH):
        u = rmsnorm(h, norm_mix[l])
        z = u @ w_in[l]
        p, q, k, v, g_ret, g_pool_br, g_ret_br = jnp.split(z, split_idx, axis=-1)
        y_pool = pool_branch(p, w_pool_group[l], pool_scale[l]) @ w_pool_proj[l]
        y_ret = retention_branch(q, k, v, g_ret, ret_gn_gain[l]) @ w_ret_proj[l]
        merged = jax.nn.sigmoid(g_pool_br) * y_pool + jax.nn.sigmoid(g_ret_br) * y_ret
        h = h + merged @ w_out[l]
        h = h + conv_glu(rmsnorm(h, norm_ffn[l]), w_up[l], conv_w[l], conv_b[l], w_down[l])
    return rmsnorm(h, norm_final)
```

```python
import functools

import numpy as np
import jax
import jax.numpy as jnp
from jax import lax
from jax.experimental import pallas as pl
from jax.experimental.pallas import tpu as pltpu

D_MODEL = 2048
BATCH = 2
SEQ = 4096
TOKENS = BATCH * SEQ

CHUNK = 64
POOL_WINDOWS = (2, 4, 8, 16)
POOL_WIDTH = D_MODEL // 2
POOL_GROUP = POOL_WIDTH // len(POOL_WINDOWS)
RET_HEADS = 8
RET_QK_DIM = 128
RET_V_DIM = 256
RET_QK_WIDTH = RET_HEADS * RET_QK_DIM
RET_V_WIDTH = RET_HEADS * RET_V_DIM
D_FF = 5632
ROPE_BASE = 10000.0
RMS_EPS = 1e-6
GN_EPS = 1e-5
IN_WIDTH = POOL_WIDTH + 2 * RET_QK_WIDTH + 2 * RET_V_WIDTH + 2 * D_MODEL

_REF_P, _REF_Q, _REF_K, _REF_V, _REF_GRET, _REF_GPOOL, _REF_GRETBR = (
    0, 1024, 2048, 3072, 5120, 7168, 9216)
_ZB_V, _ZB_GRET, _ZB_GPOOL, _ZB_GRETBR = 0, 1, 2, 3
_ZB_P, _ZB_Q, _ZB_K = 8, 9, 10

HALO = 16
SUPER = 256
VMEM_LIMIT = 56 * 1024 * 1024

BF16 = jnp.bfloat16
F32 = jnp.float32


def _dot(a, b):
    return jnp.dot(a, b, preferred_element_type=F32)


def _rms_scale(h, gain):
    ms = jnp.mean(h * h, axis=-1, keepdims=True)
    return h * lax.rsqrt(ms + RMS_EPS) * gain


def _sigmoid(x):
    return 1.0 / (1.0 + jnp.exp(-x))


def _inproj_kernel(x_ref, g_ref, w_ref, z_ref, u_ref):
    @pl.when(pl.program_id(1) == 0)
    def _():
        u_ref[...] = _rms_scale(x_ref[...], g_ref[...]).astype(BF16)

    z_ref[...] = _dot(u_ref[...], w_ref[...]).astype(z_ref.dtype)


def _in_proj(x, gain, w, *, tm=1024, tn=1024):
    m, d = x.shape
    n = w.shape[1]
    return pl.pallas_call(
        _inproj_kernel,
        out_shape=jax.ShapeDtypeStruct((m, n), BF16),
        grid=(m // tm, n // tn),
        in_specs=[
            pl.BlockSpec((tm, d), lambda i, j: (i, 0)),
            pl.BlockSpec((1, d), lambda i, j: (0, 0)),
            pl.BlockSpec((d, tn), lambda i, j: (0, j)),
        ],
        out_specs=pl.BlockSpec((tm, tn), lambda i, j: (i, j)),
        scratch_shapes=[pltpu.VMEM((tm, d), BF16)],
        compiler_params=pltpu.CompilerParams(
            dimension_semantics=("arbitrary", "arbitrary"),
            vmem_limit_bytes=VMEM_LIMIT),
        name="in_proj",
    )(x, gain, w)


def _pool_kernel(p_ref, halo_ref, wg_ref, scale_ref, pp_ref, y_ref):
    tp = p_ref.shape[0]
    pos0 = (pl.program_id(0) * tp) & (SEQ - 1)
    p = p_ref[...]
    halo = halo_ref[...]
    halo = jnp.where(pos0 == 0, jnp.zeros_like(halo), halo)

    r = lax.broadcasted_iota(jnp.int32, (tp, tp), 0)
    c = lax.broadcasted_iota(jnp.int32, (tp, tp), 1)
    rh = lax.broadcasted_iota(jnp.int32, (tp, HALO), 0)
    ch = lax.broadcasted_iota(jnp.int32, (tp, HALO), 1) - HALO
    pos = pos0 + lax.broadcasted_iota(jnp.int32, (tp, 1), 0)

    mixed = []
    for gi, w in enumerate(POOL_WINDOWS):
        sl = slice(gi * POOL_GROUP, (gi + 1) * POOL_GROUP)
        pg = p[:, sl]
        a_main = ((c <= r) & (c > r - w)).astype(BF16)
        a_halo = (ch > rh - w).astype(BF16)
        wsum = _dot(a_main, pg) + _dot(a_halo, halo[:, sl])
        cnt = jnp.minimum(pos + 1, w).astype(F32)
        pooled = wsum / cnt - pg.astype(F32)
        mg = _dot(pooled.astype(BF16), wg_ref[gi]) * scale_ref[:, sl]
        mixed.append(mg.astype(BF16))
    mixed = jnp.concatenate(mixed, axis=-1)
    y_ref[...] = _dot(mixed, pp_ref[...]).astype(y_ref.dtype)


def _pool(z, w_group, scale, w_proj, *, tp=256):
    m = z.shape[0]
    halo_blocks = tp // HALO
    return pl.pallas_call(
        _pool_kernel,
        out_shape=jax.ShapeDtypeStruct((m, D_MODEL), BF16),
        grid=(m // tp,),
        in_specs=[
            pl.BlockSpec((tp, POOL_WIDTH), lambda i: (i, _ZB_P)),
            pl.BlockSpec((HALO, POOL_WIDTH),
                         lambda i: (jnp.maximum(i * halo_blocks - 1, 0), _ZB_P)),
            pl.BlockSpec((len(POOL_WINDOWS), POOL_GROUP, POOL_GROUP), lambda i: (0, 0, 0)),
            pl.BlockSpec((1, POOL_WIDTH), lambda i: (0, 0)),
            pl.BlockSpec((POOL_WIDTH, D_MODEL), lambda i: (0, 0)),
        ],
        out_specs=pl.BlockSpec((tp, D_MODEL), lambda i: (i, 0)),
        compiler_params=pltpu.CompilerParams(
            dimension_semantics=("arbitrary",),
            vmem_limit_bytes=VMEM_LIMIT),
        name="pool",
    )(z, z, w_group, scale, w_proj)


def _retention_tables():
    half = RET_QK_DIM // 2
    inv = ROPE_BASE ** (-np.arange(half, dtype=np.float64) / half)
    ang = np.arange(SEQ, dtype=np.float64)[:, None] * inv[None, :]
    cos = np.concatenate([np.cos(ang), np.cos(ang)], axis=-1)
    sin = np.concatenate([-np.sin(ang), np.sin(ang)], axis=-1)
    log_g = np.log(1.0 - 2.0 ** (-5.0 - np.arange(RET_HEADS, dtype=np.float64)))
    r = np.arange(SUPER)
    dist = np.abs(r[:, None] - r[None, :]).astype(np.float64)
    visible = (r[None, :] // CHUNK) <= (r[:, None] // CHUNK)
    decay = np.where(visible[None], np.exp(dist[None] * log_g[:, None, None]), 0.0)
    q_decay = np.exp((r + 1.0)[None, :] * log_g[:, None])
    k_decay = np.exp((SUPER - 1.0 - r)[None, :] * log_g[:, None])
    q_decay = np.broadcast_to(q_decay[:, :, None], (RET_HEADS, SUPER, RET_QK_DIM))
    k_decay = np.broadcast_to(k_decay[:, :, None], (RET_HEADS, SUPER, RET_QK_DIM))
    state_decay = [float(v) for v in np.exp(SUPER * log_g)]
    as_f32 = lambda a: jnp.asarray(np.ascontiguousarray(a), dtype=F32)
    return as_f32(cos), as_f32(sin), as_f32(decay), as_f32(q_decay), as_f32(k_decay), state_decay


def _retention_kernel(state_decay, q_ref, k_ref, v_ref, g_ref, cos_ref, sin_ref,
                      dec_ref, qd_ref, kd_ref, gain_ref, o_ref, state_ref):
    @pl.when(pl.program_id(1) == 0)
    def _():
        state_ref[...] = jnp.zeros_like(state_ref)

    cos = cos_ref[...]
    sin = sin_ref[...]
    half = RET_QK_DIM // 2
    for h in range(RET_HEADS):
        qk = slice(h * RET_QK_DIM, (h + 1) * RET_QK_DIM)
        vv = slice(h * RET_V_DIM, (h + 1) * RET_V_DIM)
        q = q_ref[:, qk].astype(F32)
        k = k_ref[:, qk].astype(F32)
        q = q * cos + pltpu.roll(q, half, 1) * sin
        k = (k * cos + pltpu.roll(k, half, 1) * sin) * (RET_QK_DIM ** -0.5)
        v = v_ref[:, vv]
        scores = lax.dot_general(q.astype(BF16), k.astype(BF16),
                                 (((1,), (1,)), ((), ())), preferred_element_type=F32)
        scores = scores * dec_ref[h]
        state = state_ref[h]
        out = _dot(scores.astype(BF16), v) + _dot((q * qd_ref[h]).astype(BF16), state.astype(BF16))
        kv = lax.dot_general((k * kd_ref[h]).astype(BF16), v,
                             (((0,), (0,)), ((), ())), preferred_element_type=F32)
        state_ref[h] = state * state_decay[h] + kv

        mu = jnp.mean(out, axis=-1, keepdims=True)
        dev = out - mu
        var = jnp.mean(dev * dev, axis=-1, keepdims=True)
        normed = dev * lax.rsqrt(var + GN_EPS) * gain_ref[:, vv]
        g = g_ref[:, vv].astype(F32)
        o_ref[:, vv] = (g * _sigmoid(g) * normed).astype(o_ref.dtype)


def _retention(z, gn_gain):
    m = z.shape[0]
    ns = SEQ // SUPER
    cos, sin, decay, q_decay, k_decay, state_decay = _retention_tables()
    row = lambda b, s: b * ns + s
    const3 = lambda b, s: (0, 0, 0)
    return pl.pallas_call(
        functools.partial(_retention_kernel, state_decay),
        out_shape=jax.ShapeDtypeStruct((m, RET_V_WIDTH), BF16),
        grid=(BATCH, ns),
        in_specs=[
            pl.BlockSpec((SUPER, RET_QK_WIDTH), lambda b, s: (row(b, s), _ZB_Q)),
            pl.BlockSpec((SUPER, RET_QK_WIDTH), lambda b, s: (row(b, s), _ZB_K)),
            pl.BlockSpec((SUPER, RET_V_WIDTH), lambda b, s: (row(b, s), _ZB_V)),
            pl.BlockSpec((SUPER, RET_V_WIDTH), lambda b, s: (row(b, s), _ZB_GRET)),
            pl.BlockSpec((SUPER, RET_QK_DIM), lambda b, s: (s, 0)),
            pl.BlockSpec((SUPER, RET_QK_DIM), lambda b, s: (s, 0)),
            pl.BlockSpec((RET_HEADS, SUPER, SUPER), const3),
            pl.BlockSpec((RET_HEADS, SUPER, RET_QK_DIM), const3),
            pl.BlockSpec((RET_HEADS, SUPER, RET_QK_DIM), const3),
            pl.BlockSpec((1, RET_V_WIDTH), lambda b, s: (0, 0)),
        ],
        out_specs=pl.BlockSpec((SUPER, RET_V_WIDTH), lambda b, s: (row(b, s), 0)),
        scratch_shapes=[pltpu.VMEM((RET_HEADS, RET_QK_DIM, RET_V_DIM), F32)],
        compiler_params=pltpu.CompilerParams(
            dimension_semantics=("arbitrary", "arbitrary"),
            vmem_limit_bytes=VMEM_LIMIT),
        name="retention",
    )(z, z, z, z, cos, sin, decay, q_decay, k_decay, gn_gain)


def _mix_kernel(o_ref, yp_ref, gp_ref, gr_ref, x_ref, pret_ref, wout_ref, nf_ref,
                h1_ref, u2_ref):
    y_ret = _dot(o_ref[...], pret_ref[...])
    merged = (_sigmoid(gp_ref[...].astype(F32)) * yp_ref[...].astype(F32)
              + _sigmoid(gr_ref[...].astype(F32)) * y_ret)
    h1 = x_ref[...] + _dot(merged.astype(BF16), wout_ref[...])
    h1_ref[...] = h1
    u2_ref[...] = _rms_scale(h1, nf_ref[...]).astype(BF16)


def _mix_out(o, y_pool, z, x, w_ret, w_out, norm_ffn, *, tm=256):
    m = x.shape[0]
    rows = lambda i: (i, 0)
    const = lambda i: (0, 0)
    return pl.pallas_call(
        _mix_kernel,
        out_shape=(jax.ShapeDtypeStruct((m, D_MODEL), F32),
                   jax.ShapeDtypeStruct((m, D_MODEL), BF16)),
        grid=(m // tm,),
        in_specs=[
            pl.BlockSpec((tm, RET_V_WIDTH), rows),
            pl.BlockSpec((tm, D_MODEL), rows),
            pl.BlockSpec((tm, D_MODEL), lambda i: (i, _ZB_GPOOL)),
            pl.BlockSpec((tm, D_MODEL), lambda i: (i, _ZB_GRETBR)),
            pl.BlockSpec((tm, D_MODEL), rows),
            pl.BlockSpec((RET_V_WIDTH, D_MODEL), const),
            pl.BlockSpec((D_MODEL, D_MODEL), const),
            pl.BlockSpec((1, D_MODEL), const),
        ],
        out_specs=(pl.BlockSpec((tm, D_MODEL), rows), pl.BlockSpec((tm, D_MODEL), rows)),
        compiler_params=pltpu.CompilerParams(
            dimension_semantics=("arbitrary",),
            vmem_limit_bytes=VMEM_LIMIT),
        name="mix_out",
    )(o, y_pool, z, z, x, w_ret, w_out, norm_ffn)


CARRY_ROWS = 8


def _up_kernel(u_ref, wg_ref, wu_ref, cw_ref, cb_ref, act_ref, carry_ref):
    tm = u_ref.shape[0]

    @pl.when(((pl.program_id(1) * tm) & (SEQ - 1)) == 0)
    def _():
        carry_ref[...] = jnp.zeros_like(carry_ref)

    u = u_ref[...]
    gate = _dot(u, wg_ref[...])
    up = _dot(u, wu_ref[...])
    carry = carry_ref[...]
    prev1 = carry[CARRY_ROWS - 1:CARRY_ROWS, :]
    prev2 = carry[CARRY_ROWS - 2:CARRY_ROWS - 1, :]
    row = lax.broadcasted_iota(jnp.int32, gate.shape, 0)
    g1 = jnp.where(row == 0, prev1, pltpu.roll(gate, 1, 0))
    g2 = jnp.where(row == 0, prev2, jnp.where(row == 1, prev1, pltpu.roll(gate, 2, 0)))
    conv = cw_ref[0:1, :] * g2 + cw_ref[1:2, :] * g1 + cw_ref[2:3, :] * gate + cb_ref[...]
    carry_ref[...] = gate[tm - CARRY_ROWS:, :]
    act_ref[...] = (conv * _sigmoid(conv) * up).astype(act_ref.dtype)


def _up_glu(u2, w_up, conv_w, conv_b, *, tm=1024, tn=512):
    m, d = u2.shape
    nj = D_FF // tn
    return pl.pallas_call(
        _up_kernel,
        out_shape=jax.ShapeDtypeStruct((m, D_FF), BF16),
        grid=(nj, m // tm),
        in_specs=[
            pl.BlockSpec((tm, d), lambda j, i: (i, 0)),
            pl.BlockSpec((d, tn), lambda j, i: (0, j)),
            pl.BlockSpec((d, tn), lambda j, i: (0, nj + j)),
            pl.BlockSpec((3, tn), lambda j, i: (0, j)),
            pl.BlockSpec((1, tn), lambda j, i: (0, j)),
        ],
        out_specs=pl.BlockSpec((tm, tn), lambda j, i: (i, j)),
        scratch_shapes=[pltpu.VMEM((CARRY_ROWS, tn), F32)],
        compiler_params=pltpu.CompilerParams(
            dimension_semantics=("arbitrary", "arbitrary"),
            vmem_limit_bytes=VMEM_LIMIT),
        name="up_glu",
    )(u2, w_up, w_up, conv_w, conv_b)


def _down_kernel(act_ref, w_ref, h1_ref, nf_ref, out_ref, acc_ref):
    k = pl.program_id(1)

    @pl.when(k == 0)
    def _():
        acc_ref[...] = h1_ref[...]

    acc_ref[...] += _dot(act_ref[...], w_ref[...])

    @pl.when(k == pl.num_programs(1) - 1)
    def _():
        out_ref[...] = _rms_scale(acc_ref[...], nf_ref[...])


def _down_norm(act, w_down, h1, norm_final, *, tm=1024, tk=512):
    m, f = act.shape
    d = w_down.shape[1]
    return pl.pallas_call(
        _down_kernel,
        out_shape=jax.ShapeDtypeStruct((m, d), F32),
        grid=(m // tm, f // tk),
        in_specs=[
            pl.BlockSpec((tm, tk), lambda i, k: (i, k)),
            pl.BlockSpec((tk, d), lambda i, k: (k, 0)),
            pl.BlockSpec((tm, d), lambda i, k: (i, 0)),
            pl.BlockSpec((1, d), lambda i, k: (0, 0)),
        ],
        out_specs=pl.BlockSpec((tm, d), lambda i, k: (i, 0)),
        scratch_shapes=[pltpu.VMEM((tm, d), F32)],
        compiler_params=pltpu.CompilerParams(
            dimension_semantics=("arbitrary", "arbitrary"),
            vmem_limit_bytes=VMEM_LIMIT),
        name="down_norm",
    )(act, w_down, h1, norm_final)


def _permute_in_columns(w_in):
    parts = [(_REF_V, RET_V_WIDTH), (_REF_GRET, RET_V_WIDTH), (_REF_GPOOL, D_MODEL),
             (_REF_GRETBR, D_MODEL), (_REF_P, POOL_WIDTH), (_REF_Q, RET_QK_WIDTH),
             (_REF_K, RET_QK_WIDTH)]
    return jnp.concatenate([w_in[:, s:s + n] for s, n in parts], axis=1)


def kernel(x, norm_mix, w_in, w_pool_group, pool_scale, w_pool_proj, ret_gn_gain,
           w_ret_proj, w_out, norm_ffn, w_up, conv_w, conv_b, w_down, norm_final):
    assert x.shape == (BATCH, SEQ, D_MODEL) and norm_mix.shape[0] == 1
    h = x.reshape(TOKENS, D_MODEL)
    z = _in_proj(h, norm_mix, _permute_in_columns(w_in[0]).astype(BF16))
    y_pool = _pool(z, w_pool_group[0].astype(BF16), pool_scale, w_pool_proj[0].astype(BF16))
    o = _retention(z, ret_gn_gain)
    h1, u2 = _mix_out(o, y_pool, z, h, w_ret_proj[0].astype(BF16), w_out[0].astype(BF16),
                      norm_ffn)
    act = _up_glu(u2, w_up[0].astype(BF16), conv_w[0], conv_b)
    out = _down_norm(act, w_down[0].astype(BF16), h1, norm_final.reshape(1, D_MODEL))
    return out.reshape(BATCH, SEQ, D_MODEL)
```

```python
import functools

import numpy as np
import jax
import jax.numpy as jnp
from jax import lax
from jax.experimental import pallas as pl
from jax.experimental.pallas import tpu as pltpu

D_MODEL = 2048
BATCH = 2
SEQ = 4096
TOKENS = BATCH * SEQ

CHUNK = 64
POOL_WINDOWS = (2, 4, 8, 16)
POOL_WIDTH = D_MODEL // 2
POOL_GROUP = POOL_WIDTH // len(POOL_WINDOWS)
RET_HEADS = 8
RET_QK_DIM = 128
RET_V_DIM = 256
RET_QK_WIDTH = RET_HEADS * RET_QK_DIM
RET_V_WIDTH = RET_HEADS * RET_V_DIM
D_FF = 5632
ROPE_BASE = 10000.0
RMS_EPS = 1e-6
GN_EPS = 1e-5
IN_WIDTH = POOL_WIDTH + 2 * RET_QK_WIDTH + 2 * RET_V_WIDTH + 2 * D_MODEL

_REF_P, _REF_Q, _REF_K, _REF_V, _REF_GRET, _REF_GPOOL, _REF_GRETBR = (
    0, 1024, 2048, 3072, 5120, 7168, 9216)
_ZB_V, _ZB_GRET, _ZB_GPOOL, _ZB_GRETBR = 0, 1, 2, 3
_ZB_P, _ZB_Q, _ZB_K = 8, 9, 10

HALO = 16
SUPER = 256
VMEM_LIMIT = 56 * 1024 * 1024

BF16 = jnp.bfloat16
F32 = jnp.float32


def _dot(a, b):
    return jnp.dot(a, b, preferred_element_type=F32)


def _rms_scale(h, gain):
    ms = jnp.mean(h * h, axis=-1, keepdims=True)
    return h * lax.rsqrt(ms + RMS_EPS) * gain


def _sigmoid(x):
    return 1.0 / (1.0 + jnp.exp(-x))


def _prenorm_kernel(x_ref, g_ref, u_ref):
    u_ref[...] = _rms_scale(x_ref[...], g_ref[...]).astype(u_ref.dtype)


def _prenorm(x, gain, *, tm=512):
    m, d = x.shape
    return pl.pallas_call(
        _prenorm_kernel,
        out_shape=jax.ShapeDtypeStruct((m, d), BF16),
        grid=(m // tm,),
        in_specs=[pl.BlockSpec((tm, d), lambda i: (i, 0)),
                  pl.BlockSpec((1, d), lambda i: (0, 0))],
        out_specs=pl.BlockSpec((tm, d), lambda i: (i, 0)),
        compiler_params=pltpu.CompilerParams(
            dimension_semantics=("arbitrary",),
            vmem_limit_bytes=VMEM_LIMIT),
        name="prenorm",
    )(x, gain)


def _inproj_kernel(u_ref, w_ref, z_ref):
    z_ref[...] = _dot(u_ref[...], w_ref[...].astype(BF16)).astype(z_ref.dtype)


def _in_proj(u, w, *, tm=1024, tn=1024):
    m, d = u.shape
    nj = w.shape[1] // tn
    shift = _REF_V // tn
    src = lambda j: jnp.where(j < nj - shift, j + shift, j + shift - nj)
    return pl.pallas_call(
        _inproj_kernel,
        out_shape=jax.ShapeDtypeStruct((m, w.shape[1]), BF16),
        grid=(nj, m // tm),
        in_specs=[
            pl.BlockSpec((tm, d), lambda j, i: (i, 0)),
            pl.BlockSpec((d, tn), lambda j, i: (0, src(j))),
        ],
        out_specs=pl.BlockSpec((tm, tn), lambda j, i: (i, j)),
        compiler_params=pltpu.CompilerParams(
            dimension_semantics=("arbitrary", "arbitrary"),
            vmem_limit_bytes=VMEM_LIMIT),
        name="in_proj",
    )(u, w)


def _pool_kernel(p_ref, halo_ref, wg_ref, scale_ref, pp_ref, y_ref):
    tp = p_ref.shape[0]
    pos0 = (pl.program_id(0) * tp) & (SEQ - 1)
    p = p_ref[...]
    halo = halo_ref[...]
    halo = jnp.where(pos0 == 0, jnp.zeros_like(halo), halo)

    r = lax.broadcasted_iota(jnp.int32, (tp, tp), 0)
    c = lax.broadcasted_iota(jnp.int32, (tp, tp), 1)
    rh = lax.broadcasted_iota(jnp.int32, (tp, HALO), 0)
    ch = lax.broadcasted_iota(jnp.int32, (tp, HALO), 1) - HALO
    pos = pos0 + lax.broadcasted_iota(jnp.int32, (tp, 1), 0)

    mixed = []
    for gi, w in enumerate(POOL_WINDOWS):
        sl = slice(gi * POOL_GROUP, (gi + 1) * POOL_GROUP)
        pg = p[:, sl]
        a_main = ((c <= r) & (c > r - w)).astype(BF16)
        a_halo = (ch > rh - w).astype(BF16)
        wsum = _dot(a_main, pg) + _dot(a_halo, halo[:, sl])
        cnt = jnp.minimum(pos + 1, w).astype(F32)
        pooled = wsum / cnt - pg.astype(F32)
        mg = _dot(pooled.astype(BF16), wg_ref[gi]) * scale_ref[:, sl]
        mixed.append(mg.astype(BF16))
    mixed = jnp.concatenate(mixed, axis=-1)
    y_ref[...] = _dot(mixed, pp_ref[...]).astype(y_ref.dtype)


def _pool(z, w_group, scale, w_proj, *, tp=256):
    m = z.shape[0]
    halo_blocks = tp // HALO
    return pl.pallas_call(
        _pool_kernel,
        out_shape=jax.ShapeDtypeStruct((m, D_MODEL), BF16),
        grid=(m // tp,),
        in_specs=[
            pl.BlockSpec((tp, POOL_WIDTH), lambda i: (i, _ZB_P)),
            pl.BlockSpec((HALO, POOL_WIDTH),
                         lambda i: (jnp.maximum(i * halo_blocks - 1, 0), _ZB_P)),
            pl.BlockSpec((len(POOL_WINDOWS), POOL_GROUP, POOL_GROUP), lambda i: (0, 0, 0)),
            pl.BlockSpec((1, POOL_WIDTH), lambda i: (0, 0)),
            pl.BlockSpec((POOL_WIDTH, D_MODEL), lambda i: (0, 0)),
        ],
        out_specs=pl.BlockSpec((tp, D_MODEL), lambda i: (i, 0)),
        compiler_params=pltpu.CompilerParams(
            dimension_semantics=("arbitrary",),
            vmem_limit_bytes=VMEM_LIMIT),
        name="pool",
    )(z, z, w_group, scale, w_proj)


def _retention_tables():
    half = RET_QK_DIM // 2
    inv = ROPE_BASE ** (-np.arange(half, dtype=np.float64) / half)
    ang = np.arange(SEQ, dtype=np.float64)[:, None] * inv[None, :]
    cos = np.concatenate([np.cos(ang), np.cos(ang)], axis=-1)
    sin = np.concatenate([-np.sin(ang), np.sin(ang)], axis=-1)
    log_g = np.log(1.0 - 2.0 ** (-5.0 - np.arange(RET_HEADS, dtype=np.float64)))
    r = np.arange(SUPER)
    dist = np.abs(r[:, None] - r[None, :]).astype(np.float64)
    visible = (r[None, :] // CHUNK) <= (r[:, None] // CHUNK)
    decay = np.where(visible[None], np.exp(dist[None] * log_g[:, None, None]), 0.0)
    q_decay = np.exp((r + 1.0)[None, :] * log_g[:, None])
    k_decay = np.exp((SUPER - 1.0 - r)[None, :] * log_g[:, None])
    q_decay = np.broadcast_to(q_decay[:, :, None], (RET_HEADS, SUPER, RET_QK_DIM))
    k_decay = np.broadcast_to(k_decay[:, :, None], (RET_HEADS, SUPER, RET_QK_DIM))
    state_decay = [float(v) for v in np.exp(SUPER * log_g)]
    as_f32 = lambda a: jnp.asarray(np.ascontiguousarray(a), dtype=F32)
    return as_f32(cos), as_f32(sin), as_f32(decay), as_f32(q_decay), as_f32(k_decay), state_decay


def _retention_kernel(state_decay, q_ref, k_ref, v_ref, g_ref, cos_ref, sin_ref,
                      dec_ref, qd_ref, kd_ref, gain_ref, o_ref, state_ref):
    @pl.when(pl.program_id(1) == 0)
    def _():
        state_ref[...] = jnp.zeros_like(state_ref)

    cos = cos_ref[...]
    sin = sin_ref[...]
    half = RET_QK_DIM // 2
    for h in range(RET_HEADS):
        qk = slice(h * RET_QK_DIM, (h + 1) * RET_QK_DIM)
        vv = slice(h * RET_V_DIM, (h + 1) * RET_V_DIM)
        q = q_ref[:, qk].astype(F32)
        k = k_ref[:, qk].astype(F32)
        q = q * cos + pltpu.roll(q, half, 1) * sin
        k = (k * cos + pltpu.roll(k, half, 1) * sin) * (RET_QK_DIM ** -0.5)
        v = v_ref[:, vv]
        scores = lax.dot_general(q.astype(BF16), k.astype(BF16),
                                 (((1,), (1,)), ((), ())), preferred_element_type=F32)
        scores = scores * dec_ref[h]
        state = state_ref[h]
        out = _dot(scores.astype(BF16), v) + _dot((q * qd_ref[h]).astype(BF16), state.astype(BF16))
        kv = lax.dot_general((k * kd_ref[h]).astype(BF16), v,
                             (((0,), (0,)), ((), ())), preferred_element_type=F32)
        state_ref[h] = state * state_decay[h] + kv

        mu = jnp.mean(out, axis=-1, keepdims=True)
        dev = out - mu
        var = jnp.mean(dev * dev, axis=-1, keepdims=True)
        normed = dev * lax.rsqrt(var + GN_EPS) * gain_ref[:, vv]
        g = g_ref[:, vv].astype(F32)
        o_ref[:, vv] = (g * _sigmoid(g) * normed).astype(o_ref.dtype)


def _retention(z, gn_gain):
    m = z.shape[0]
    ns = SEQ // SUPER
    cos, sin, decay, q_decay, k_decay, state_decay = _retention_tables()
    row = lambda b, s: b * ns + s
    const3 = lambda b, s: (0, 0, 0)
    return pl.pallas_call(
        functools.partial(_retention_kernel, state_decay),
        out_shape=jax.ShapeDtypeStruct((m, RET_V_WIDTH), BF16),
        grid=(BATCH, ns),
        in_specs=[
            pl.BlockSpec((SUPER, RET_QK_WIDTH), lambda b, s: (row(b, s), _ZB_Q)),
            pl.BlockSpec((SUPER, RET_QK_WIDTH), lambda b, s: (row(b, s), _ZB_K)),
            pl.BlockSpec((SUPER, RET_V_WIDTH), lambda b, s: (row(b, s), _ZB_V)),
            pl.BlockSpec((SUPER, RET_V_WIDTH), lambda b, s: (row(b, s), _ZB_GRET)),
            pl.BlockSpec((SUPER, RET_QK_DIM), lambda b, s: (s, 0)),
            pl.BlockSpec((SUPER, RET_QK_DIM), lambda b, s: (s, 0)),
            pl.BlockSpec((RET_HEADS, SUPER, SUPER), const3),
            pl.BlockSpec((RET_HEADS, SUPER, RET_QK_DIM), const3),
            pl.BlockSpec((RET_HEADS, SUPER, RET_QK_DIM), const3),
            pl.BlockSpec((1, RET_V_WIDTH), lambda b, s: (0, 0)),
        ],
        out_specs=pl.BlockSpec((SUPER, RET_V_WIDTH), lambda b, s: (row(b, s), 0)),
        scratch_shapes=[pltpu.VMEM((RET_HEADS, RET_QK_DIM, RET_V_DIM), F32)],
        compiler_params=pltpu.CompilerParams(
            dimension_semantics=("arbitrary", "arbitrary"),
            vmem_limit_bytes=VMEM_LIMIT),
        name="retention",
    )(z, z, z, z, cos, sin, decay, q_decay, k_decay, gn_gain)


def _mix_kernel(o_ref, yp_ref, gp_ref, gr_ref, x_ref, pret_ref, wout_ref, nf_ref,
                h1_ref, u2_ref):
    y_ret = _dot(o_ref[...], pret_ref[...])
    merged = (_sigmoid(gp_ref[...].astype(F32)) * yp_ref[...].astype(F32)
              + _sigmoid(gr_ref[...].astype(F32)) * y_ret)
    h1 = x_ref[...] + _dot(merged.astype(BF16), wout_ref[...])
    h1_ref[...] = h1
    u2_ref[...] = _rms_scale(h1, nf_ref[...]).astype(BF16)


def _mix_out(o, y_pool, z, x, w_ret, w_out, norm_ffn, *, tm=256):
    m = x.shape[0]
    rows = lambda i: (i, 0)
    const = lambda i: (0, 0)
    resident = functools.partial(pl.BlockSpec, index_map=const, pipeline_mode=pl.Buffered(1))
    return pl.pallas_call(
        _mix_kernel,
        out_shape=(jax.ShapeDtypeStruct((m, D_MODEL), F32),
                   jax.ShapeDtypeStruct((m, D_MODEL), BF16)),
        grid=(m // tm,),
        in_specs=[
            pl.BlockSpec((tm, RET_V_WIDTH), rows),
            pl.BlockSpec((tm, D_MODEL), rows),
            pl.BlockSpec((tm, D_MODEL), lambda i: (i, _ZB_GPOOL)),
            pl.BlockSpec((tm, D_MODEL), lambda i: (i, _ZB_GRETBR)),
            pl.BlockSpec((tm, D_MODEL), rows),
            resident((RET_V_WIDTH, D_MODEL)),
            resident((D_MODEL, D_MODEL)),
            pl.BlockSpec((1, D_MODEL), const),
        ],
        out_specs=(pl.BlockSpec((tm, D_MODEL), rows), pl.BlockSpec((tm, D_MODEL), rows)),
        compiler_params=pltpu.CompilerParams(
            dimension_semantics=("arbitrary",),
            vmem_limit_bytes=VMEM_LIMIT),
        name="mix_out",
    )(o, y_pool, z, z, x, w_ret, w_out, norm_ffn)


CARRY_ROWS = 8


def _up_kernel(u_ref, wg_ref, wu_ref, cw_ref, cb_ref, act_ref, carry_ref):
    tm = u_ref.shape[0]

    @pl.when(((pl.program_id(1) * tm) & (SEQ - 1)) == 0)
    def _():
        carry_ref[...] = jnp.zeros_like(carry_ref)

    u = u_ref[...]
    gate = _dot(u, wg_ref[...].astype(BF16))
    up = _dot(u, wu_ref[...].astype(BF16))
    carry = carry_ref[...]
    prev1 = carry[CARRY_ROWS - 1:CARRY_ROWS, :]
    prev2 = carry[CARRY_ROWS - 2:CARRY_ROWS - 1, :]
    row = lax.broadcasted_iota(jnp.int32, gate.shape, 0)
    g1 = jnp.where(row == 0, prev1, pltpu.roll(gate, 1, 0))
    g2 = jnp.where(row == 0, prev2, jnp.where(row == 1, prev1, pltpu.roll(gate, 2, 0)))
    conv = cw_ref[0:1, :] * g2 + cw_ref[1:2, :] * g1 + cw_ref[2:3, :] * gate + cb_ref[...]
    carry_ref[...] = gate[tm - CARRY_ROWS:, :]
    act_ref[...] = (conv * _sigmoid(conv) * up).astype(act_ref.dtype)


def _up_glu(u2, w_up, conv_w, conv_b, *, tm=1024, tn=512):
    m, d = u2.shape
    nj = D_FF // tn
    return pl.pallas_call(
        _up_kernel,
        out_shape=jax.ShapeDtypeStruct((m, D_FF), BF16),
        grid=(nj, m // tm),
        in_specs=[
            pl.BlockSpec((tm, d), lambda j, i: (i, 0)),
            pl.BlockSpec((d, tn), lambda j, i: (0, j)),
            pl.BlockSpec((d, tn), lambda j, i: (0, nj + j)),
            pl.BlockSpec((3, tn), lambda j, i: (0, j)),
            pl.BlockSpec((1, tn), lambda j, i: (0, j)),
        ],
        out_specs=pl.BlockSpec((tm, tn), lambda j, i: (i, j)),
        scratch_shapes=[pltpu.VMEM((CARRY_ROWS, tn), F32)],
        compiler_params=pltpu.CompilerParams(
            dimension_semantics=("arbitrary", "arbitrary"),
            vmem_limit_bytes=VMEM_LIMIT),
        name="up_glu",
    )(u2, w_up, w_up, conv_w, conv_b)


def _down_kernel(act_ref, w_ref, h1_ref, nf_ref, out_ref, acc_ref):
    k = pl.program_id(1)

    @pl.when(k == 0)
    def _():
        acc_ref[...] = h1_ref[...]

    acc_ref[...] += _dot(act_ref[...], w_ref[...].astype(BF16))

    @pl.when(k == pl.num_programs(1) - 1)
    def _():
        out_ref[...] = _rms_scale(acc_ref[...], nf_ref[...])


def _down_norm(act, w_down, h1, norm_final, *, tm=1024, tk=512):
    m, f = act.shape
    d = w_down.shape[1]
    return pl.pallas_call(
        _down_kernel,
        out_shape=jax.ShapeDtypeStruct((m, d), F32),
        grid=(m // tm, f // tk),
        in_specs=[
            pl.BlockSpec((tm, tk), lambda i, k: (i, k)),
            pl.BlockSpec((tk, d), lambda i, k: (k, 0)),
            pl.BlockSpec((tm, d), lambda i, k: (i, 0)),
            pl.BlockSpec((1, d), lambda i, k: (0, 0)),
        ],
        out_specs=pl.BlockSpec((tm, d), lambda i, k: (i, 0)),
        scratch_shapes=[pltpu.VMEM((tm, d), F32)],
        compiler_params=pltpu.CompilerParams(
            dimension_semantics=("arbitrary", "arbitrary"),
            vmem_limit_bytes=VMEM_LIMIT),
        name="down_norm",
    )(act, w_down, h1, norm_final)


def kernel(x, norm_mix, w_in, w_pool_group, pool_scale, w_pool_proj, ret_gn_gain,
           w_ret_proj, w_out, norm_ffn, w_up, conv_w, conv_b, w_down, norm_final):
    assert x.shape == (BATCH, SEQ, D_MODEL) and norm_mix.shape[0] == 1
    h = x.reshape(TOKENS, D_MODEL)
    z = _in_proj(_prenorm(h, norm_mix), w_in[0])
    y_pool = _pool(z, w_pool_group[0].astype(BF16), pool_scale, w_pool_proj[0].astype(BF16))
    o = _retention(z, ret_gn_gain)
    h1, u2 = _mix_out(o, y_pool, z, h, w_ret_proj[0].astype(BF16), w_out[0].astype(BF16),
                      norm_ffn)
    act = _up_glu(u2, w_up[0], conv_w[0], conv_b)
    out = _down_norm(act, w_down[0], h1, norm_final.reshape(1, D_MODEL))
    return out.reshape(BATCH, SEQ, D_MODEL)
```

```python
import functools

import numpy as np
import jax
import jax.numpy as jnp
from jax import lax
from jax.experimental import pallas as pl
from jax.experimental.pallas import tpu as pltpu

D_MODEL = 2048
BATCH = 2
SEQ = 4096
TOKENS = BATCH * SEQ

CHUNK = 64
POOL_WINDOWS = (2, 4, 8, 16)
POOL_WIDTH = D_MODEL // 2
POOL_GROUP = POOL_WIDTH // len(POOL_WINDOWS)
RET_HEADS = 8
RET_QK_DIM = 128
RET_V_DIM = 256
RET_QK_WIDTH = RET_HEADS * RET_QK_DIM
RET_V_WIDTH = RET_HEADS * RET_V_DIM
D_FF = 5632
ROPE_BASE = 10000.0
RMS_EPS = 1e-6
GN_EPS = 1e-5

_REF_V = 3072
_ZB_V, _ZB_GRET, _ZB_GPOOL, _ZB_GRETBR = 0, 1, 2, 3
_ZB_P, _ZB_Q, _ZB_K = 8, 9, 10

HALO = 16
SUPER = 256
VMEM_LIMIT = 56 * 1024 * 1024

BF16 = jnp.bfloat16
F32 = jnp.float32


def _dot(a, b):
    return jnp.dot(a, b, preferred_element_type=F32)


def _rms_scale(h, gain):
    ms = jnp.mean(h * h, axis=-1, keepdims=True)
    return h * lax.rsqrt(ms + RMS_EPS) * gain


def _sigmoid(x):
    return 1.0 / (1.0 + jnp.exp(-x))


def _prenorm_kernel(x_ref, g_ref, u_ref):
    u_ref[...] = _rms_scale(x_ref[...], g_ref[...]).astype(u_ref.dtype)


def _prenorm(x, gain, *, tm=512):
    m, d = x.shape
    return pl.pallas_call(
        _prenorm_kernel,
        out_shape=jax.ShapeDtypeStruct((m, d), BF16),
        grid=(m // tm,),
        in_specs=[pl.BlockSpec((tm, d), lambda i: (i, 0)),
                  pl.BlockSpec((1, d), lambda i: (0, 0))],
        out_specs=pl.BlockSpec((tm, d), lambda i: (i, 0)),
        compiler_params=pltpu.CompilerParams(
            dimension_semantics=("arbitrary",),
            vmem_limit_bytes=VMEM_LIMIT),
        name="prenorm",
    )(x, gain)


def _inproj_kernel(u_ref, w_ref, z_ref):
    z_ref[...] = _dot(u_ref[...], w_ref[...].astype(BF16)).astype(z_ref.dtype)


def _in_proj(u, w, *, tm=2048, tn=1024):
    m, d = u.shape
    nj = w.shape[1] // tn
    shift = _REF_V // tn
    src = lambda j: jnp.where(j < nj - shift, j + shift, j + shift - nj)
    return pl.pallas_call(
        _inproj_kernel,
        out_shape=jax.ShapeDtypeStruct((m, w.shape[1]), BF16),
        grid=(nj, m // tm),
        in_specs=[
            pl.BlockSpec((tm, d), lambda j, i: (i, 0)),
            pl.BlockSpec((d, tn), lambda j, i: (0, src(j))),
        ],
        out_specs=pl.BlockSpec((tm, tn), lambda j, i: (i, j)),
        compiler_params=pltpu.CompilerParams(
            dimension_semantics=("arbitrary", "arbitrary"),
            vmem_limit_bytes=VMEM_LIMIT),
        name="in_proj",
    )(u, w)


def _retention_tables():
    half = RET_QK_DIM // 2
    inv = ROPE_BASE ** (-np.arange(half, dtype=np.float64) / half)
    ang = np.arange(SEQ, dtype=np.float64)[:, None] * inv[None, :]
    cos = np.concatenate([np.cos(ang), np.cos(ang)], axis=-1)
    sin = np.concatenate([-np.sin(ang), np.sin(ang)], axis=-1)
    log_g = np.log(1.0 - 2.0 ** (-5.0 - np.arange(RET_HEADS, dtype=np.float64)))
    r = np.arange(SUPER)
    dist = np.abs(r[:, None] - r[None, :]).astype(np.float64)
    visible = (r[None, :] // CHUNK) <= (r[:, None] // CHUNK)
    decay = np.where(visible[None], np.exp(dist[None] * log_g[:, None, None]), 0.0)
    q_decay = np.exp((r + 1.0)[None, :] * log_g[:, None])
    k_decay = np.exp((SUPER - 1.0 - r)[None, :] * log_g[:, None])
    q_decay = np.broadcast_to(q_decay[:, :, None], (RET_HEADS, SUPER, RET_QK_DIM))
    k_decay = np.broadcast_to(k_decay[:, :, None], (RET_HEADS, SUPER, RET_QK_DIM))
    state_decay = [float(v) for v in np.exp(SUPER * log_g)]
    as_f32 = lambda a: jnp.asarray(np.ascontiguousarray(a), dtype=F32)
    return as_f32(cos), as_f32(sin), as_f32(decay), as_f32(q_decay), as_f32(k_decay), state_decay


def _retention_heads(state_decay, q_ref, k_ref, v_ref, g_ref, cos_ref, sin_ref,
                     dec_ref, qd_ref, kd_ref, gain_ref, state_ref):
    cos = cos_ref[...]
    sin = sin_ref[...]
    half = RET_QK_DIM // 2
    heads = []
    for h in range(RET_HEADS):
        qk = slice(h * RET_QK_DIM, (h + 1) * RET_QK_DIM)
        vv = slice(h * RET_V_DIM, (h + 1) * RET_V_DIM)
        q = q_ref[:, qk].astype(F32)
        k = k_ref[:, qk].astype(F32)
        q = q * cos + pltpu.roll(q, half, 1) * sin
        k = (k * cos + pltpu.roll(k, half, 1) * sin) * (RET_QK_DIM ** -0.5)
        v = v_ref[:, vv]
        scores = lax.dot_general(q.astype(BF16), k.astype(BF16),
                                 (((1,), (1,)), ((), ())), preferred_element_type=F32)
        scores = scores * dec_ref[h]
        state = state_ref[h]
        out = _dot(scores.astype(BF16), v) + _dot((q * qd_ref[h]).astype(BF16), state.astype(BF16))
        kv = lax.dot_general((k * kd_ref[h]).astype(BF16), v,
                             (((0,), (0,)), ((), ())), preferred_element_type=F32)
        state_ref[h] = state * state_decay[h] + kv

        mu = jnp.mean(out, axis=-1, keepdims=True)
        dev = out - mu
        var = jnp.mean(dev * dev, axis=-1, keepdims=True)
        normed = dev * lax.rsqrt(var + GN_EPS) * gain_ref[:, vv]
        g = g_ref[:, vv].astype(F32)
        heads.append((g * _sigmoid(g) * normed).astype(BF16))
    return jnp.concatenate(heads, axis=-1)


def _pool_mixed(pos0, p_ref, halo_ref, wg_ref, scale_ref):
    tp = p_ref.shape[0]
    p = p_ref[...]
    halo = halo_ref[...]
    halo = jnp.where(pos0 == 0, jnp.zeros_like(halo), halo)
    r = lax.broadcasted_iota(jnp.int32, (tp, tp), 0)
    c = lax.broadcasted_iota(jnp.int32, (tp, tp), 1)
    rh = lax.broadcasted_iota(jnp.int32, (tp, HALO), 0)
    ch = lax.broadcasted_iota(jnp.int32, (tp, HALO), 1) - HALO
    pos = pos0 + lax.broadcasted_iota(jnp.int32, (tp, 1), 0)
    mixed = []
    for gi, w in enumerate(POOL_WINDOWS):
        sl = slice(gi * POOL_GROUP, (gi + 1) * POOL_GROUP)
        pg = p[:, sl]
        a_main = ((c <= r) & (c > r - w)).astype(BF16)
        a_halo = (ch > rh - w).astype(BF16)
        wsum = _dot(a_main, pg) + _dot(a_halo, halo[:, sl])
        cnt = jnp.minimum(pos + 1, w).astype(F32)
        pooled = wsum / cnt - pg.astype(F32)
        mg = _dot(pooled.astype(BF16), wg_ref[gi]) * scale_ref[:, sl]
        mixed.append(mg.astype(BF16))
    return jnp.concatenate(mixed, axis=-1)


def _mixer_kernel(state_decay, q_ref, k_ref, v_ref, g_ref, gp_ref, gr_ref, p_ref, halo_ref, x_ref,
                  cos_ref, sin_ref, dec_ref, qd_ref, kd_ref, gain_ref,
                  wg_ref, scale_ref, pp_ref, pret_ref, wout_ref, nf_ref,
                  h1_ref, u2_ref, state_ref):
    s = pl.program_id(1)

    @pl.when(s == 0)
    def _():
        state_ref[...] = jnp.zeros_like(state_ref)

    o = _retention_heads(state_decay, q_ref, k_ref, v_ref, g_ref, cos_ref, sin_ref,
                         dec_ref, qd_ref, kd_ref, gain_ref, state_ref)
    y_ret = _dot(o, pret_ref[...])
    mixed = _pool_mixed(s * SUPER, p_ref, halo_ref, wg_ref, scale_ref)
    y_pool = _dot(mixed, pp_ref[...])
    merged = (_sigmoid(gp_ref[...].astype(F32)) * y_pool
              + _sigmoid(gr_ref[...].astype(F32)) * y_ret)
    h1 = x_ref[...] + _dot(merged.astype(BF16), wout_ref[...])
    h1_ref[...] = h1
    u2_ref[...] = _rms_scale(h1, nf_ref[...]).astype(BF16)


def _mixer(z, x, gn_gain, w_group, pool_scale, w_pool_proj, w_ret, w_out, norm_ffn):
    m = x.shape[0]
    ns = SEQ // SUPER
    halo_per_tile = SUPER // HALO
    cos, sin, decay, q_decay, k_decay, state_decay = _retention_tables()
    row = lambda b, s: b * ns + s
    tile = lambda width, col: pl.BlockSpec((SUPER, width), lambda b, s: (row(b, s), col))
    resident = lambda shape: pl.BlockSpec(shape, lambda b, s: (0,) * len(shape),
                                          pipeline_mode=pl.Buffered(1))
    return pl.pallas_call(
        functools.partial(_mixer_kernel, state_decay),
        out_shape=(jax.ShapeDtypeStruct((m, D_MODEL), F32),
                   jax.ShapeDtypeStruct((m, D_MODEL), BF16)),
        grid=(BATCH, ns),
        in_specs=[
            tile(RET_QK_WIDTH, _ZB_Q),
            tile(RET_QK_WIDTH, _ZB_K),
            tile(RET_V_WIDTH, _ZB_V),
            tile(RET_V_WIDTH, _ZB_GRET),
            tile(D_MODEL, _ZB_GPOOL),
            tile(D_MODEL, _ZB_GRETBR),
            tile(POOL_WIDTH, _ZB_P),
            pl.BlockSpec((HALO, POOL_WIDTH),
                         lambda b, s: (jnp.maximum(row(b, s) * halo_per_tile - 1, 0), _ZB_P)),
            tile(D_MODEL, 0),
            pl.BlockSpec((SUPER, RET_QK_DIM), lambda b, s: (s, 0)),
            pl.BlockSpec((SUPER, RET_QK_DIM), lambda b, s: (s, 0)),
            resident((RET_HEADS, SUPER, SUPER)),
            resident((RET_HEADS, SUPER, RET_QK_DIM)),
            resident((RET_HEADS, SUPER, RET_QK_DIM)),
            resident((1, RET_V_WIDTH)),
            resident((len(POOL_WINDOWS), POOL_GROUP, POOL_GROUP)),
            resident((1, POOL_WIDTH)),
            resident((POOL_WIDTH, D_MODEL)),
            resident((RET_V_WIDTH, D_MODEL)),
            resident((D_MODEL, D_MODEL)),
            resident((1, D_MODEL)),
        ],
        out_specs=(tile(D_MODEL, 0), tile(D_MODEL, 0)),
        scratch_shapes=[pltpu.VMEM((RET_HEADS, RET_QK_DIM, RET_V_DIM), F32)],
        compiler_params=pltpu.CompilerParams(
            dimension_semantics=("arbitrary", "arbitrary"),
            vmem_limit_bytes=VMEM_LIMIT),
        name="mixer",
    )(z, z, z, z, z, z, z, z, x, cos, sin, decay, q_decay, k_decay, gn_gain,
      w_group, pool_scale, w_pool_proj, w_ret, w_out, norm_ffn)


CARRY_ROWS = 8


def _up_kernel(u_ref, wg_ref, wu_ref, cw_ref, cb_ref, act_ref, carry_ref):
    tm = u_ref.shape[0]

    @pl.when(((pl.program_id(1) * tm) & (SEQ - 1)) == 0)
    def _():
        carry_ref[...] = jnp.zeros_like(carry_ref)

    u = u_ref[...]
    gate = _dot(u, wg_ref[...].astype(BF16))
    up = _dot(u, wu_ref[...].astype(BF16))
    carry = carry_ref[...]
    prev1 = carry[CARRY_ROWS - 1:CARRY_ROWS, :]
    prev2 = carry[CARRY_ROWS - 2:CARRY_ROWS - 1, :]
    row = lax.broadcasted_iota(jnp.int32, gate.shape, 0)
    g1 = jnp.where(row == 0, prev1, pltpu.roll(gate, 1, 0))
    g2 = jnp.where(row == 0, prev2, jnp.where(row == 1, prev1, pltpu.roll(gate, 2, 0)))
    conv = cw_ref[0:1, :] * g2 + cw_ref[1:2, :] * g1 + cw_ref[2:3, :] * gate + cb_ref[...]
    carry_ref[...] = gate[tm - CARRY_ROWS:, :]
    act_ref[...] = (conv * _sigmoid(conv) * up).astype(act_ref.dtype)


def _up_glu(u2, w_up, conv_w, conv_b, *, tm=1024, tn=512):
    m, d = u2.shape
    nj = D_FF // tn
    return pl.pallas_call(
        _up_kernel,
        out_shape=jax.ShapeDtypeStruct((m, D_FF), BF16),
        grid=(nj, m // tm),
        in_specs=[
            pl.BlockSpec((tm, d), lambda j, i: (i, 0)),
            pl.BlockSpec((d, tn), lambda j, i: (0, j)),
            pl.BlockSpec((d, tn), lambda j, i: (0, nj + j)),
            pl.BlockSpec((3, tn), lambda j, i: (0, j)),
            pl.BlockSpec((1, tn), lambda j, i: (0, j)),
        ],
        out_specs=pl.BlockSpec((tm, tn), lambda j, i: (i, j)),
        scratch_shapes=[pltpu.VMEM((CARRY_ROWS, tn), F32)],
        compiler_params=pltpu.CompilerParams(
            dimension_semantics=("arbitrary", "arbitrary"),
            vmem_limit_bytes=VMEM_LIMIT),
        name="up_glu",
    )(u2, w_up, w_up, conv_w, conv_b)


def _down_kernel(act_ref, w_ref, h1_ref, nf_ref, out_ref, acc_ref):
    k = pl.program_id(1)

    @pl.when(k == 0)
    def _():
        acc_ref[...] = h1_ref[...]

    acc_ref[...] += _dot(act_ref[...], w_ref[...].astype(BF16))

    @pl.when(k == pl.num_programs(1) - 1)
    def _():
        out_ref[...] = _rms_scale(acc_ref[...], nf_ref[...])


def _down_norm(act, w_down, h1, norm_final, *, tm=1024, tk=512):
    m, f = act.shape
    d = w_down.shape[1]
    return pl.pallas_call(
        _down_kernel,
        out_shape=jax.ShapeDtypeStruct((m, d), F32),
        grid=(m // tm, f // tk),
        in_specs=[
            pl.BlockSpec((tm, tk), lambda i, k: (i, k)),
            pl.BlockSpec((tk, d), lambda i, k: (k, 0)),
            pl.BlockSpec((tm, d), lambda i, k: (i, 0)),
            pl.BlockSpec((1, d), lambda i, k: (0, 0)),
        ],
        out_specs=pl.BlockSpec((tm, d), lambda i, k: (i, 0)),
        scratch_shapes=[pltpu.VMEM((tm, d), F32)],
        compiler_params=pltpu.CompilerParams(
            dimension_semantics=("arbitrary", "arbitrary"),
            vmem_limit_bytes=VMEM_LIMIT),
        name="down_norm",
    )(act, w_down, h1, norm_final)


def kernel(x, norm_mix, w_in, w_pool_group, pool_scale, w_pool_proj, ret_gn_gain,
           w_ret_proj, w_out, norm_ffn, w_up, conv_w, conv_b, w_down, norm_final):
    assert x.shape == (BATCH, SEQ, D_MODEL) and norm_mix.shape[0] == 1
    h = x.reshape(TOKENS, D_MODEL)
    z = _in_proj(_prenorm(h, norm_mix), w_in[0])
    h1, u2 = _mixer(z, h, ret_gn_gain, w_pool_group[0].astype(BF16), pool_scale,
                    w_pool_proj[0].astype(BF16), w_ret_proj[0].astype(BF16),
                    w_out[0].astype(BF16), norm_ffn)
    act = _up_glu(u2, w_up[0], conv_w[0], conv_b)
    out = _down_norm(act, w_down[0], h1, norm_final.reshape(1, D_MODEL))
    return out.reshape(BATCH, SEQ, D_MODEL)
```

```python
import functools

import numpy as np
import jax
import jax.numpy as jnp
from jax import lax
from jax.experimental import pallas as pl
from jax.experimental.pallas import tpu as pltpu

D_MODEL = 2048
BATCH = 2
SEQ = 4096
TOKENS = BATCH * SEQ

CHUNK = 64
POOL_WINDOWS = (2, 4, 8, 16)
POOL_WIDTH = D_MODEL // 2
POOL_GROUP = POOL_WIDTH // len(POOL_WINDOWS)
RET_HEADS = 8
RET_QK_DIM = 128
RET_V_DIM = 256
RET_QK_WIDTH = RET_HEADS * RET_QK_DIM
RET_V_WIDTH = RET_HEADS * RET_V_DIM
D_FF = 5632
ROPE_BASE = 10000.0
RMS_EPS = 1e-6
GN_EPS = 1e-5

IN_BLOCK = 1024
_REF_V_BLOCK, _REF_Q_BLOCK, _N_WIDE_BLOCKS = 3, 1, 8
_ZB_V, _ZB_GRET, _ZB_GPOOL, _ZB_GRETBR = 0, 1, 2, 3
_ZB_Q, _ZB_K = 8, 9

HALO = 16
SUPER = 256
VMEM_LIMIT = 56 * 1024 * 1024

BF16 = jnp.bfloat16
F32 = jnp.float32


def _dot(a, b):
    return jnp.dot(a, b, preferred_element_type=F32)


def _rms_scale(h, gain):
    ms = jnp.mean(h * h, axis=-1, keepdims=True)
    return h * lax.rsqrt(ms + RMS_EPS) * gain


def _sigmoid(x):
    return 1.0 / (1.0 + jnp.exp(-x))


def _prenorm_pool_kernel(x_ref, g_ref, w_ref, u_ref, zp_ref):
    u = _rms_scale(x_ref[...], g_ref[...]).astype(u_ref.dtype)
    u_ref[...] = u
    zp_ref[...] = _dot(u, w_ref[...].astype(BF16)).astype(zp_ref.dtype)


def _prenorm_pool(x, gain, w_in, *, tm=1024):
    m, d = x.shape
    return pl.pallas_call(
        _prenorm_pool_kernel,
        out_shape=(jax.ShapeDtypeStruct((m, d), BF16),
                   jax.ShapeDtypeStruct((m, POOL_WIDTH), BF16)),
        grid=(m // tm,),
        in_specs=[pl.BlockSpec((tm, d), lambda i: (i, 0)),
                  pl.BlockSpec((1, d), lambda i: (0, 0)),
                  pl.BlockSpec((d, POOL_WIDTH), lambda i: (0, 0), pipeline_mode=pl.Buffered(1))],
        out_specs=(pl.BlockSpec((tm, d), lambda i: (i, 0)),
                   pl.BlockSpec((tm, POOL_WIDTH), lambda i: (i, 0))),
        compiler_params=pltpu.CompilerParams(
            dimension_semantics=("arbitrary",),
            vmem_limit_bytes=VMEM_LIMIT),
        name="prenorm_pool",
    )(x, gain, w_in)


CAST_BLOCKS = 32


def _inproj_kernel(u_ref, w_ref, *cast_refs):
    n_cast = (len(cast_refs) - 1) // 2
    z_ref = cast_refs[n_cast]
    z_ref[...] = _dot(u_ref[...], w_ref[...].astype(BF16)).astype(z_ref.dtype)
    for src_ref, dst_ref in zip(cast_refs[:n_cast], cast_refs[n_cast + 1:]):
        dst_ref[...] = src_ref[...].astype(dst_ref.dtype)


def _in_proj(u, w, later_weights, *, tm=2048):
    m, d = u.shape
    nj, ni = w.shape[1] // IN_BLOCK - 1, m // tm
    assert nj * ni >= CAST_BLOCKS
    src = lambda j: jnp.where(j < _N_WIDE_BLOCKS, j + _REF_V_BLOCK, j - _N_WIDE_BLOCKS + _REF_Q_BLOCK)
    cast_block = lambda j, i: (jnp.minimum(j * ni + i, CAST_BLOCKS - 1), 0)
    cast_specs = [pl.BlockSpec((cw.shape[0] // CAST_BLOCKS, cw.shape[1]), cast_block)
                  for cw in later_weights]
    outs = pl.pallas_call(
        _inproj_kernel,
        out_shape=[jax.ShapeDtypeStruct((m, nj * IN_BLOCK), BF16)]
                  + [jax.ShapeDtypeStruct(cw.shape, BF16) for cw in later_weights],
        grid=(nj, ni),
        in_specs=[
            pl.BlockSpec((tm, d), lambda j, i: (i, 0)),
            pl.BlockSpec((d, IN_BLOCK), lambda j, i: (0, src(j))),
        ] + cast_specs,
        out_specs=[pl.BlockSpec((tm, IN_BLOCK), lambda j, i: (i, j))] + cast_specs,
        compiler_params=pltpu.CompilerParams(
            dimension_semantics=("arbitrary", "arbitrary"),
            vmem_limit_bytes=VMEM_LIMIT),
        name="in_proj",
    )(u, w, *later_weights)
    return outs[0], outs[1:]


def _retention_tables():
    half = RET_QK_DIM // 2
    inv = ROPE_BASE ** (-np.arange(half, dtype=np.float64) / half)
    ang = np.arange(SEQ, dtype=np.float64)[:, None] * inv[None, :]
    cos = np.concatenate([np.cos(ang), np.cos(ang)], axis=-1)
    sin = np.concatenate([-np.sin(ang), np.sin(ang)], axis=-1)
    log_g = np.log(1.0 - 2.0 ** (-5.0 - np.arange(RET_HEADS, dtype=np.float64)))
    r = np.arange(SUPER)
    dist = np.abs(r[:, None] - r[None, :]).astype(np.float64)
    visible = (r[None, :] // CHUNK) <= (r[:, None] // CHUNK)
    decay = np.where(visible[None], np.exp(dist[None] * log_g[:, None, None]), 0.0)
    q_decay = np.exp((r + 1.0)[None, :] * log_g[:, None])
    k_decay = np.exp((SUPER - 1.0 - r)[None, :] * log_g[:, None])
    q_decay = np.broadcast_to(q_decay[:, :, None], (RET_HEADS, SUPER, RET_QK_DIM))
    k_decay = np.broadcast_to(k_decay[:, :, None], (RET_HEADS, SUPER, RET_QK_DIM))
    state_decay = [float(v) for v in np.exp(SUPER * log_g)]
    as_f32 = lambda a: jnp.asarray(np.ascontiguousarray(a), dtype=F32)
    return as_f32(cos), as_f32(sin), as_f32(decay), as_f32(q_decay), as_f32(k_decay), state_decay


def _retention_heads(state_decay, q_ref, k_ref, v_ref, g_ref, cos_ref, sin_ref,
                     dec_ref, qd_ref, kd_ref, gain_ref, state_ref):
    cos = cos_ref[...]
    sin = sin_ref[...]
    half = RET_QK_DIM // 2
    heads = []
    for h in range(RET_HEADS):
        qk = slice(h * RET_QK_DIM, (h + 1) * RET_QK_DIM)
        vv = slice(h * RET_V_DIM, (h + 1) * RET_V_DIM)
        q = q_ref[:, qk].astype(F32)
        k = k_ref[:, qk].astype(F32)
        q = q * cos + pltpu.roll(q, half, 1) * sin
        k = (k * cos + pltpu.roll(k, half, 1) * sin) * (RET_QK_DIM ** -0.5)
        v = v_ref[:, vv]
        scores = lax.dot_general(q.astype(BF16), k.astype(BF16),
                                 (((1,), (1,)), ((), ())), preferred_element_type=F32)
        scores = scores * dec_ref[h]
        state = state_ref[h]
        out = _dot(scores.astype(BF16), v) + _dot((q * qd_ref[h]).astype(BF16), state.astype(BF16))
        kv = lax.dot_general((k * kd_ref[h]).astype(BF16), v,
                             (((0,), (0,)), ((), ())), preferred_element_type=F32)
        state_ref[h] = state * state_decay[h] + kv

        mu = jnp.mean(out, axis=-1, keepdims=True)
        dev = out - mu
        var = jnp.mean(dev * dev, axis=-1, keepdims=True)
        normed = dev * lax.rsqrt(var + GN_EPS) * gain_ref[:, vv]
        g = g_ref[:, vv].astype(F32)
        heads.append((g * _sigmoid(g) * normed).astype(BF16))
    return jnp.concatenate(heads, axis=-1)


def _pool_mixed(pos0, p_ref, halo_ref, wg_ref, scale_ref):
    tp = p_ref.shape[0]
    p = p_ref[...]
    halo = halo_ref[...]
    halo = jnp.where(pos0 == 0, jnp.zeros_like(halo), halo)
    r = lax.broadcasted_iota(jnp.int32, (tp, tp), 0)
    c = lax.broadcasted_iota(jnp.int32, (tp, tp), 1)
    rh = lax.broadcasted_iota(jnp.int32, (tp, HALO), 0)
    ch = lax.broadcasted_iota(jnp.int32, (tp, HALO), 1) - HALO
    pos = pos0 + lax.broadcasted_iota(jnp.int32, (tp, 1), 0)
    mixed = []
    for gi, w in enumerate(POOL_WINDOWS):
        sl = slice(gi * POOL_GROUP, (gi + 1) * POOL_GROUP)
        pg = p[:, sl]
        a_main = ((c <= r) & (c > r - w)).astype(BF16)
        a_halo = (ch > rh - w).astype(BF16)
        wsum = _dot(a_main, pg) + _dot(a_halo, halo[:, sl])
        cnt = jnp.minimum(pos + 1, w).astype(F32)
        pooled = wsum / cnt - pg.astype(F32)
        mg = _dot(pooled.astype(BF16), wg_ref[gi]) * scale_ref[:, sl]
        mixed.append(mg.astype(BF16))
    return jnp.concatenate(mixed, axis=-1)


def _mixer_kernel(state_decay, q_ref, k_ref, v_ref, g_ref, gp_ref, gr_ref, p_ref, halo_ref, x_ref,
                  cos_ref, sin_ref, dec_ref, qd_ref, kd_ref, gain_ref,
                  wg_ref, scale_ref, pp_ref, pret_ref, wout_ref, nf_ref,
                  h1_ref, u2_ref, state_ref):
    s = pl.program_id(1)

    @pl.when(s == 0)
    def _():
        state_ref[...] = jnp.zeros_like(state_ref)

    o = _retention_heads(state_decay, q_ref, k_ref, v_ref, g_ref, cos_ref, sin_ref,
                         dec_ref, qd_ref, kd_ref, gain_ref, state_ref)
    y_ret = _dot(o, pret_ref[...])
    mixed = _pool_mixed(s * SUPER, p_ref, halo_ref, wg_ref, scale_ref)
    y_pool = _dot(mixed, pp_ref[...])
    merged = (_sigmoid(gp_ref[...].astype(F32)) * y_pool
              + _sigmoid(gr_ref[...].astype(F32)) * y_ret)
    h1 = x_ref[...] + _dot(merged.astype(BF16), wout_ref[...])
    h1_ref[...] = h1
    u2_ref[...] = _rms_scale(h1, nf_ref[...]).astype(BF16)


def _mixer(z, zp, x, gn_gain, w_group, pool_scale, w_pool_proj, w_ret, w_out, norm_ffn):
    m = x.shape[0]
    ns = SEQ // SUPER
    halo_per_tile = SUPER // HALO
    cos, sin, decay, q_decay, k_decay, state_decay = _retention_tables()
    row = lambda b, s: b * ns + s
    tile = lambda width, col: pl.BlockSpec((SUPER, width), lambda b, s: (row(b, s), col))
    resident = lambda shape: pl.BlockSpec(shape, lambda b, s: (0,) * len(shape),
                                          pipeline_mode=pl.Buffered(1))
    return pl.pallas_call(
        functools.partial(_mixer_kernel, state_decay),
        out_shape=(jax.ShapeDtypeStruct((m, D_MODEL), F32),
                   jax.ShapeDtypeStruct((m, D_MODEL), BF16)),
        grid=(BATCH, ns),
        in_specs=[
            tile(RET_QK_WIDTH, _ZB_Q),
            tile(RET_QK_WIDTH, _ZB_K),
            tile(RET_V_WIDTH, _ZB_V),
            tile(RET_V_WIDTH, _ZB_GRET),
            tile(D_MODEL, _ZB_GPOOL),
            tile(D_MODEL, _ZB_GRETBR),
            tile(POOL_WIDTH, 0),
            pl.BlockSpec((HALO, POOL_WIDTH),
                         lambda b, s: (jnp.maximum(row(b, s) * halo_per_tile - 1, 0), 0)),
            tile(D_MODEL, 0),
            pl.BlockSpec((SUPER, RET_QK_DIM), lambda b, s: (s, 0)),
            pl.BlockSpec((SUPER, RET_QK_DIM), lambda b, s: (s, 0)),
            resident((RET_HEADS, SUPER, SUPER)),
            resident((RET_HEADS, SUPER, RET_QK_DIM)),
            resident((RET_HEADS, SUPER, RET_QK_DIM)),
            resident((1, RET_V_WIDTH)),
            resident((len(POOL_WINDOWS), POOL_GROUP, POOL_GROUP)),
            resident((1, POOL_WIDTH)),
            resident((POOL_WIDTH, D_MODEL)),
            resident((RET_V_WIDTH, D_MODEL)),
            resident((D_MODEL, D_MODEL)),
            resident((1, D_MODEL)),
        ],
        out_specs=(tile(D_MODEL, 0), tile(D_MODEL, 0)),
        scratch_shapes=[pltpu.VMEM((RET_HEADS, RET_QK_DIM, RET_V_DIM), F32)],
        compiler_params=pltpu.CompilerParams(
            dimension_semantics=("arbitrary", "arbitrary"),
            vmem_limit_bytes=VMEM_LIMIT),
        name="mixer",
    )(z, z, z, z, z, z, zp, zp, x, cos, sin, decay, q_decay, k_decay, gn_gain,
      w_group, pool_scale, w_pool_proj, w_ret, w_out, norm_ffn)


CARRY_ROWS = 8


def _up_kernel(u_ref, wg_ref, wu_ref, cw_ref, cb_ref, act_ref, carry_ref):
    tm = u_ref.shape[0]

    @pl.when(((pl.program_id(1) * tm) & (SEQ - 1)) == 0)
    def _():
        carry_ref[...] = jnp.zeros_like(carry_ref)

    u = u_ref[...]
    gate = _dot(u, wg_ref[...].astype(BF16))
    up = _dot(u, wu_ref[...].astype(BF16))
    carry = carry_ref[...]
    prev1 = carry[CARRY_ROWS - 1:CARRY_ROWS, :]
    prev2 = carry[CARRY_ROWS - 2:CARRY_ROWS - 1, :]
    row = lax.broadcasted_iota(jnp.int32, gate.shape, 0)
    g1 = jnp.where(row == 0, prev1, pltpu.roll(gate, 1, 0))
    g2 = jnp.where(row == 0, prev2, jnp.where(row == 1, prev1, pltpu.roll(gate, 2, 0)))
    conv = cw_ref[0:1, :] * g2 + cw_ref[1:2, :] * g1 + cw_ref[2:3, :] * gate + cb_ref[...]
    carry_ref[...] = gate[tm - CARRY_ROWS:, :]
    act_ref[...] = (conv * _sigmoid(conv) * up).astype(act_ref.dtype)


def _up_glu(u2, w_up, conv_w, conv_b, *, tm=1024, tn=512):
    m, d = u2.shape
    nj = D_FF // tn
    return pl.pallas_call(
        _up_kernel,
        out_shape=jax.ShapeDtypeStruct((m, D_FF), BF16),
        grid=(nj, m // tm),
        in_specs=[
            pl.BlockSpec((tm, d), lambda j, i: (i, 0)),
            pl.BlockSpec((d, tn), lambda j, i: (0, j)),
            pl.BlockSpec((d, tn), lambda j, i: (0, nj + j)),
            pl.BlockSpec((3, tn), lambda j, i: (0, j)),
            pl.BlockSpec((1, tn), lambda j, i: (0, j)),
        ],
        out_specs=pl.BlockSpec((tm, tn), lambda j, i: (i, j)),
        scratch_shapes=[pltpu.VMEM((CARRY_ROWS, tn), F32)],
        compiler_params=pltpu.CompilerParams(
            dimension_semantics=("arbitrary", "arbitrary"),
            vmem_limit_bytes=VMEM_LIMIT),
        name="up_glu",
    )(u2, w_up, w_up, conv_w, conv_b)


def _down_kernel(act_ref, w_ref, h1_ref, nf_ref, out_ref, acc_ref):
    k = pl.program_id(1)

    @pl.when(k == 0)
    def _():
        acc_ref[...] = h1_ref[...]

    acc_ref[...] += _dot(act_ref[...], w_ref[...].astype(BF16))

    @pl.when(k == pl.num_programs(1) - 1)
    def _():
        out_ref[...] = _rms_scale(acc_ref[...], nf_ref[...])


def _down_norm(act, w_down, h1, norm_final, *, tm=1024, tk=512):
    m, f = act.shape
    d = w_down.shape[1]
    return pl.pallas_call(
        _down_kernel,
        out_shape=jax.ShapeDtypeStruct((m, d), F32),
        grid=(m // tm, f // tk),
        in_specs=[
            pl.BlockSpec((tm, tk), lambda i, k: (i, k)),
            pl.BlockSpec((tk, d), lambda i, k: (k, 0)),
            pl.BlockSpec((tm, d), lambda i, k: (i, 0)),
            pl.BlockSpec((1, d), lambda i, k: (0, 0)),
        ],
        out_specs=pl.BlockSpec((tm, d), lambda i, k: (i, 0)),
        scratch_shapes=[pltpu.VMEM((tm, d), F32)],
        compiler_params=pltpu.CompilerParams(
            dimension_semantics=("arbitrary", "arbitrary"),
            vmem_limit_bytes=VMEM_LIMIT),
        name="down_norm",
    )(act, w_down, h1, norm_final)


def kernel(x, norm_mix, w_in, w_pool_group, pool_scale, w_pool_proj, ret_gn_gain,
           w_ret_proj, w_out, norm_ffn, w_up, conv_w, conv_b, w_down, norm_final):
    assert x.shape == (BATCH, SEQ, D_MODEL) and norm_mix.shape[0] == 1
    h = x.reshape(TOKENS, D_MODEL)
    u, zp = _prenorm_pool(h, norm_mix, w_in[0])
    z, (w_pool_proj_b, w_ret_b, w_out_b) = _in_proj(
        u, w_in[0], (w_pool_proj[0], w_ret_proj[0], w_out[0]))
    h1, u2 = _mixer(z, zp, h, ret_gn_gain, w_pool_group[0].astype(BF16), pool_scale,
                    w_pool_proj_b, w_ret_b, w_out_b, norm_ffn)
    act = _up_glu(u2, w_up[0], conv_w[0], conv_b)
    out = _down_norm(act, w_down[0], h1, norm_final.reshape(1, D_MODEL))
    return out.reshape(BATCH, SEQ, D_MODEL)
```

```python
import functools

import numpy as np
import jax
import jax.numpy as jnp
from jax import lax
from jax.experimental import pallas as pl
from jax.experimental.pallas import tpu as pltpu

D_MODEL = 2048
BATCH = 2
SEQ = 4096
TOKENS = BATCH * SEQ

CHUNK = 64
POOL_WINDOWS = (2, 4, 8, 16)
POOL_WIDTH = D_MODEL // 2
POOL_GROUP = POOL_WIDTH // len(POOL_WINDOWS)
RET_HEADS = 8
RET_QK_DIM = 128
RET_V_DIM = 256
RET_QK_WIDTH = RET_HEADS * RET_QK_DIM
RET_V_WIDTH = RET_HEADS * RET_V_DIM
D_FF = 5632
ROPE_BASE = 10000.0
RMS_EPS = 1e-6
GN_EPS = 1e-5

NARROW_WIDTH = POOL_WIDTH + 2 * RET_QK_WIDTH
_ZN_P, _ZN_Q, _ZN_K = 0, 1, 2
_ZW_V, _ZW_GRET, _ZW_GPOOL, _ZW_GRETBR = 0, 1, 2, 3

HALO = 16
SUPER = 256
VMEM_LIMIT = 56 * 1024 * 1024

BF16 = jnp.bfloat16
F32 = jnp.float32


def _dot(a, b):
    return jnp.dot(a, b, preferred_element_type=F32)


def _rms_scale(h, gain):
    ms = jnp.mean(h * h, axis=-1, keepdims=True)
    return h * lax.rsqrt(ms + RMS_EPS) * gain


def _sigmoid(x):
    return 1.0 / (1.0 + jnp.exp(-x))


def _prenorm_narrow_kernel(x_ref, g_ref, w_ref, u_ref, zn_ref):
    u = _rms_scale(x_ref[...], g_ref[...]).astype(u_ref.dtype)
    u_ref[...] = u
    zn_ref[...] = _dot(u, w_ref[...].astype(BF16)).astype(zn_ref.dtype)


def _prenorm_narrow(x, gain, w_in, *, tm=512):
    m, d = x.shape
    return pl.pallas_call(
        _prenorm_narrow_kernel,
        out_shape=(jax.ShapeDtypeStruct((m, d), BF16),
                   jax.ShapeDtypeStruct((m, NARROW_WIDTH), BF16)),
        grid=(m // tm,),
        in_specs=[pl.BlockSpec((tm, d), lambda i: (i, 0)),
                  pl.BlockSpec((1, d), lambda i: (0, 0)),
                  pl.BlockSpec((d, NARROW_WIDTH), lambda i: (0, 0), pipeline_mode=pl.Buffered(1))],
        out_specs=(pl.BlockSpec((tm, d), lambda i: (i, 0)),
                   pl.BlockSpec((tm, NARROW_WIDTH), lambda i: (i, 0))),
        compiler_params=pltpu.CompilerParams(
            dimension_semantics=("arbitrary",),
            vmem_limit_bytes=VMEM_LIMIT),
        name="prenorm_narrow",
    )(x, gain, w_in)


CAST_BLOCKS = 32


def _inproj_kernel(u_ref, w_ref, *cast_refs):
    n_cast = (len(cast_refs) - 1) // 2
    z_ref = cast_refs[n_cast]
    z_ref[...] = _dot(u_ref[...], w_ref[...].astype(BF16)).astype(z_ref.dtype)
    for src_ref, dst_ref in zip(cast_refs[:n_cast], cast_refs[n_cast + 1:]):
        dst_ref[...] = src_ref[...].astype(dst_ref.dtype)


def _in_proj(u, w, later_weights, *, tm=2048, tn=1024):
    m, d = u.shape
    skip = NARROW_WIDTH // tn
    nj, ni = w.shape[1] // tn - skip, m // tm
    assert nj * ni >= CAST_BLOCKS
    cast_block = lambda j, i: (jnp.minimum(j * ni + i, CAST_BLOCKS - 1), 0)
    cast_specs = [pl.BlockSpec((cw.shape[0] // CAST_BLOCKS, cw.shape[1]), cast_block)
                  for cw in later_weights]
    outs = pl.pallas_call(
        _inproj_kernel,
        out_shape=[jax.ShapeDtypeStruct((m, nj * tn), BF16)]
                  + [jax.ShapeDtypeStruct(cw.shape, BF16) for cw in later_weights],
        grid=(nj, ni),
        in_specs=[
            pl.BlockSpec((tm, d), lambda j, i: (i, 0)),
            pl.BlockSpec((d, tn), lambda j, i: (0, j + skip)),
        ] + cast_specs,
        out_specs=[pl.BlockSpec((tm, tn), lambda j, i: (i, j))] + cast_specs,
        compiler_params=pltpu.CompilerParams(
            dimension_semantics=("arbitrary", "arbitrary"),
            vmem_limit_bytes=VMEM_LIMIT),
        name="in_proj",
    )(u, w, *later_weights)
    return outs[0], outs[1:]


def _retention_tables():
    half = RET_QK_DIM // 2
    inv = ROPE_BASE ** (-np.arange(half, dtype=np.float64) / half)
    ang = np.arange(SEQ, dtype=np.float64)[:, None] * inv[None, :]
    cos = np.concatenate([np.cos(ang), np.cos(ang)], axis=-1)
    sin = np.concatenate([-np.sin(ang), np.sin(ang)], axis=-1)
    log_g = np.log(1.0 - 2.0 ** (-5.0 - np.arange(RET_HEADS, dtype=np.float64)))
    r = np.arange(SUPER)
    dist = np.abs(r[:, None] - r[None, :]).astype(np.float64)
    visible = (r[None, :] // CHUNK) <= (r[:, None] // CHUNK)
    decay = np.where(visible[None], np.exp(dist[None] * log_g[:, None, None]), 0.0)
    q_decay = np.exp((r + 1.0)[None, :] * log_g[:, None])
    k_decay = np.exp((SUPER - 1.0 - r)[None, :] * log_g[:, None])
    q_decay = np.broadcast_to(q_decay[:, :, None], (RET_HEADS, SUPER, RET_QK_DIM))
    k_decay = np.broadcast_to(k_decay[:, :, None], (RET_HEADS, SUPER, RET_QK_DIM))
    state_decay = [float(v) for v in np.exp(SUPER * log_g)]
    as_f32 = lambda a: jnp.asarray(np.ascontiguousarray(a), dtype=F32)
    return as_f32(cos), as_f32(sin), as_f32(decay), as_f32(q_decay), as_f32(k_decay), state_decay


def _retention_heads(state_decay, q_ref, k_ref, v_ref, g_ref, cos_ref, sin_ref,
                     dec_ref, qd_ref, kd_ref, gain_ref, state_ref):
    cos = cos_ref[...]
    sin = sin_ref[...]
    half = RET_QK_DIM // 2
    heads = []
    for h in range(RET_HEADS):
        qk = slice(h * RET_QK_DIM, (h + 1) * RET_QK_DIM)
        vv = slice(h * RET_V_DIM, (h + 1) * RET_V_DIM)
        q = q_ref[:, qk].astype(F32)
        k = k_ref[:, qk].astype(F32)
        q = q * cos + pltpu.roll(q, half, 1) * sin
        k = (k * cos + pltpu.roll(k, half, 1) * sin) * (RET_QK_DIM ** -0.5)
        v = v_ref[:, vv]
        scores = lax.dot_general(q.astype(BF16), k.astype(BF16),
                                 (((1,), (1,)), ((), ())), preferred_element_type=F32)
        scores = scores * dec_ref[h]
        state = state_ref[h]
        out = _dot(scores.astype(BF16), v) + _dot((q * qd_ref[h]).astype(BF16), state.astype(BF16))
        kv = lax.dot_general((k * kd_ref[h]).astype(BF16), v,
                             (((0,), (0,)), ((), ())), preferred_element_type=F32)
        state_ref[h] = state * state_decay[h] + kv

        mu = jnp.mean(out, axis=-1, keepdims=True)
        dev = out - mu
        var = jnp.mean(dev * dev, axis=-1, keepdims=True)
        normed = dev * lax.rsqrt(var + GN_EPS) * gain_ref[:, vv]
        g = g_ref[:, vv].astype(F32)
        heads.append((g * _sigmoid(g) * normed).astype(BF16))
    return jnp.concatenate(heads, axis=-1)


def _pool_mixed(pos0, p_ref, halo_ref, wg_ref, scale_ref):
    tp = p_ref.shape[0]
    p = p_ref[...]
    halo = halo_ref[...]
    halo = jnp.where(pos0 == 0, jnp.zeros_like(halo), halo)
    r = lax.broadcasted_iota(jnp.int32, (tp, tp), 0)
    c = lax.broadcasted_iota(jnp.int32, (tp, tp), 1)
    rh = lax.broadcasted_iota(jnp.int32, (HALO, HALO), 0)
    ch = lax.broadcasted_iota(jnp.int32, (HALO, HALO), 1) - HALO
    pos = pos0 + lax.broadcasted_iota(jnp.int32, (tp, 1), 0)
    mixed = []
    for gi, w in enumerate(POOL_WINDOWS):
        sl = slice(gi * POOL_GROUP, (gi + 1) * POOL_GROUP)
        pg = p[:, sl]
        a_main = ((c <= r) & (c > r - w)).astype(BF16)
        a_halo = (ch > rh - w).astype(BF16)
        wsum = _dot(a_main, pg)
        wsum = jnp.concatenate([wsum[:HALO] + _dot(a_halo, halo[:, sl]), wsum[HALO:]], axis=0)
        cnt = jnp.minimum(pos + 1, w).astype(F32)
        pooled = wsum / cnt - pg.astype(F32)
        mg = _dot(pooled.astype(BF16), wg_ref[gi]) * scale_ref[:, sl]
        mixed.append(mg.astype(BF16))
    return jnp.concatenate(mixed, axis=-1)


def _mixer_kernel(state_decay, q_ref, k_ref, v_ref, g_ref, gp_ref, gr_ref, p_ref, halo_ref, x_ref,
                  cos_ref, sin_ref, dec_ref, qd_ref, kd_ref, gain_ref,
                  wg_ref, scale_ref, pp_ref, pret_ref, wout_ref, nf_ref,
                  h1_ref, u2_ref, state_ref):
    s = pl.program_id(1)

    @pl.when(s == 0)
    def _():
        state_ref[...] = jnp.zeros_like(state_ref)

    o = _retention_heads(state_decay, q_ref, k_ref, v_ref, g_ref, cos_ref, sin_ref,
                         dec_ref, qd_ref, kd_ref, gain_ref, state_ref)
    y_ret = _dot(o, pret_ref[...])
    mixed = _pool_mixed(s * SUPER, p_ref, halo_ref, wg_ref, scale_ref)
    y_pool = _dot(mixed, pp_ref[...])
    merged = (_sigmoid(gp_ref[...].astype(F32)) * y_pool
              + _sigmoid(gr_ref[...].astype(F32)) * y_ret)
    h1 = x_ref[...] + _dot(merged.astype(BF16), wout_ref[...])
    h1_ref[...] = h1
    u2_ref[...] = _rms_scale(h1, nf_ref[...]).astype(BF16)


def _mixer(zw, zn, x, gn_gain, w_group, pool_scale, w_pool_proj, w_ret, w_out, norm_ffn):
    m = x.shape[0]
    ns = SEQ // SUPER
    halo_per_tile = SUPER // HALO
    cos, sin, decay, q_decay, k_decay, state_decay = _retention_tables()
    row = lambda b, s: b * ns + s
    tile = lambda width, col: pl.BlockSpec((SUPER, width), lambda b, s: (row(b, s), col))
    resident = lambda shape: pl.BlockSpec(shape, lambda b, s: (0,) * len(shape),
                                          pipeline_mode=pl.Buffered(1))
    return pl.pallas_call(
        functools.partial(_mixer_kernel, state_decay),
        out_shape=(jax.ShapeDtypeStruct((m, D_MODEL), F32),
                   jax.ShapeDtypeStruct((m, D_MODEL), BF16)),
        grid=(BATCH, ns),
        in_specs=[
            tile(RET_QK_WIDTH, _ZN_Q),
            tile(RET_QK_WIDTH, _ZN_K),
            tile(RET_V_WIDTH, _ZW_V),
            tile(RET_V_WIDTH, _ZW_GRET),
            tile(D_MODEL, _ZW_GPOOL),
            tile(D_MODEL, _ZW_GRETBR),
            tile(POOL_WIDTH, _ZN_P),
            pl.BlockSpec((HALO, POOL_WIDTH),
                         lambda b, s: (jnp.maximum(row(b, s) * halo_per_tile - 1, 0), _ZN_P)),
            tile(D_MODEL, 0),
            pl.BlockSpec((SUPER, RET_QK_DIM), lambda b, s: (s, 0)),
            pl.BlockSpec((SUPER, RET_QK_DIM), lambda b, s: (s, 0)),
            resident((RET_HEADS, SUPER, SUPER)),
            resident((RET_HEADS, SUPER, RET_QK_DIM)),
            resident((RET_HEADS, SUPER, RET_QK_DIM)),
            resident((1, RET_V_WIDTH)),
            resident((len(POOL_WINDOWS), POOL_GROUP, POOL_GROUP)),
            resident((1, POOL_WIDTH)),
            resident((POOL_WIDTH, D_MODEL)),
            resident((RET_V_WIDTH, D_MODEL)),
            resident((D_MODEL, D_MODEL)),
            resident((1, D_MODEL)),
        ],
        out_specs=(tile(D_MODEL, 0), tile(D_MODEL, 0)),
        scratch_shapes=[pltpu.VMEM((RET_HEADS, RET_QK_DIM, RET_V_DIM), F32)],
        compiler_params=pltpu.CompilerParams(
            dimension_semantics=("arbitrary", "arbitrary"),
            vmem_limit_bytes=VMEM_LIMIT),
        name="mixer",
    )(zn, zn, zw, zw, zw, zw, zn, zn, x, cos, sin, decay, q_decay, k_decay, gn_gain,
      w_group, pool_scale, w_pool_proj, w_ret, w_out, norm_ffn)


CARRY_ROWS = 8


def _up_kernel(u_ref, wg_ref, wu_ref, cw_ref, cb_ref, wd_ref, act_ref, wd_bf16_ref, carry_ref):
    tm = u_ref.shape[0]
    wd_bf16_ref[...] = wd_ref[...].astype(wd_bf16_ref.dtype)

    @pl.when(((pl.program_id(1) * tm) & (SEQ - 1)) == 0)
    def _():
        carry_ref[...] = jnp.zeros_like(carry_ref)

    wg = wg_ref[...].astype(BF16)
    wu = wu_ref[...].astype(BF16)
    carry = carry_ref[...]
    prev1 = carry[CARRY_ROWS - 1:CARRY_ROWS, :]
    prev2 = carry[CARRY_ROWS - 2:CARRY_ROWS - 1, :]
    row = lax.broadcasted_iota(jnp.int32, (UP_SUB_ROWS, act_ref.shape[1]), 0)
    for r in range(tm // UP_SUB_ROWS):
        rows = slice(r * UP_SUB_ROWS, (r + 1) * UP_SUB_ROWS)
        u = u_ref[rows, :]
        gate = _dot(u, wg)
        up = _dot(u, wu)
        g1 = jnp.where(row == 0, prev1, pltpu.roll(gate, 1, 0))
        g2 = jnp.where(row == 0, prev2, jnp.where(row == 1, prev1, pltpu.roll(gate, 2, 0)))
        conv = cw_ref[0:1, :] * g2 + cw_ref[1:2, :] * g1 + cw_ref[2:3, :] * gate + cb_ref[...]
        act_ref[rows, :] = (conv * _sigmoid(conv) * up).astype(act_ref.dtype)
        prev1 = gate[UP_SUB_ROWS - 1:UP_SUB_ROWS, :]
        prev2 = gate[UP_SUB_ROWS - 2:UP_SUB_ROWS - 1, :]
    carry_ref[...] = gate[UP_SUB_ROWS - CARRY_ROWS:, :]


UP_SUB_ROWS = 1024


def _up_glu(u2, w_up, conv_w, conv_b, w_down, *, tm=2048, tn=512):
    m, d = u2.shape
    nj, ni = D_FF // tn, m // tm
    cast_rows = w_down.shape[0] // (nj * ni)
    assert cast_rows * nj * ni == w_down.shape[0]
    cast_spec = pl.BlockSpec((cast_rows, w_down.shape[1]), lambda j, i: (j * ni + i, 0))
    return pl.pallas_call(
        _up_kernel,
        out_shape=(jax.ShapeDtypeStruct((m, D_FF), BF16),
                   jax.ShapeDtypeStruct(w_down.shape, BF16)),
        grid=(nj, ni),
        in_specs=[
            pl.BlockSpec((tm, d), lambda j, i: (i, 0)),
            pl.BlockSpec((d, tn), lambda j, i: (0, j)),
            pl.BlockSpec((d, tn), lambda j, i: (0, nj + j)),
            pl.BlockSpec((3, tn), lambda j, i: (0, j)),
            pl.BlockSpec((1, tn), lambda j, i: (0, j)),
            cast_spec,
        ],
        out_specs=(pl.BlockSpec((tm, tn), lambda j, i: (i, j)), cast_spec),
        scratch_shapes=[pltpu.VMEM((CARRY_ROWS, tn), F32)],
        compiler_params=pltpu.CompilerParams(
            dimension_semantics=("arbitrary", "arbitrary"),
            vmem_limit_bytes=VMEM_LIMIT),
        name="up_glu",
    )(u2, w_up, w_up, conv_w, conv_b, w_down)


def _down_kernel(act_ref, w_ref, h1_ref, nf_ref, out_ref, acc_ref):
    k = pl.program_id(1)

    @pl.when(k == 0)
    def _():
        acc_ref[...] = h1_ref[...]

    acc_ref[...] += _dot(act_ref[...], w_ref[...])

    @pl.when(k == pl.num_programs(1) - 1)
    def _():
        out_ref[...] = _rms_scale(acc_ref[...], nf_ref[...])


def _down_norm(act, w_down, h1, norm_final, *, tm=1024, tk=512):
    m, f = act.shape
    d = w_down.shape[1]
    return pl.pallas_call(
        _down_kernel,
        out_shape=jax.ShapeDtypeStruct((m, d), F32),
        grid=(m // tm, f // tk),
        in_specs=[
            pl.BlockSpec((tm, tk), lambda i, k: (i, k)),
            pl.BlockSpec((tk, d), lambda i, k: (k, 0)),
            pl.BlockSpec((tm, d), lambda i, k: (i, 0)),
            pl.BlockSpec((1, d), lambda i, k: (0, 0)),
        ],
        out_specs=pl.BlockSpec((tm, d), lambda i, k: (i, 0)),
        scratch_shapes=[pltpu.VMEM((tm, d), F32)],
        compiler_params=pltpu.CompilerParams(
            dimension_semantics=("arbitrary", "arbitrary"),
            vmem_limit_bytes=VMEM_LIMIT),
        name="down_norm",
    )(act, w_down, h1, norm_final)


def kernel(x, norm_mix, w_in, w_pool_group, pool_scale, w_pool_proj, ret_gn_gain,
           w_ret_proj, w_out, norm_ffn, w_up, conv_w, conv_b, w_down, norm_final):
    assert x.shape == (BATCH, SEQ, D_MODEL) and norm_mix.shape[0] == 1
    h = x.reshape(TOKENS, D_MODEL)
    u, zn = _prenorm_narrow(h, norm_mix, w_in[0])
    zw, (w_pool_proj_b, w_ret_b, w_out_b) = _in_proj(
        u, w_in[0], (w_pool_proj[0], w_ret_proj[0], w_out[0]))
    h1, u2 = _mixer(zw, zn, h, ret_gn_gain, w_pool_group[0].astype(BF16), pool_scale,
                    w_pool_proj_b, w_ret_b, w_out_b, norm_ffn)
    act, w_down_b = _up_glu(u2, w_up[0], conv_w[0], conv_b, w_down[0])
    out = _down_norm(act, w_down_b, h1, norm_final.reshape(1, D_MODEL))
    return out.reshape(BATCH, SEQ, D_MODEL)
```

```python
import functools

import numpy as np
import jax
import jax.numpy as jnp
from jax import lax
from jax.experimental import pallas as pl
from jax.experimental.pallas import tpu as pltpu

D_MODEL = 2048
BATCH = 2
SEQ = 4096
TOKENS = BATCH * SEQ

CHUNK = 64
POOL_WINDOWS = (2, 4, 8, 16)
POOL_WIDTH = D_MODEL // 2
POOL_GROUP = POOL_WIDTH // len(POOL_WINDOWS)
RET_HEADS = 8
RET_QK_DIM = 128
RET_V_DIM = 256
RET_QK_WIDTH = RET_HEADS * RET_QK_DIM
RET_V_WIDTH = RET_HEADS * RET_V_DIM
D_FF = 5632
ROPE_BASE = 10000.0
RMS_EPS = 1e-6
GN_EPS = 1e-5

NARROW_WIDTH = POOL_WIDTH + 2 * RET_QK_WIDTH
_ZN_P, _ZN_Q, _ZN_K = 0, 1, 2
_ZW_V, _ZW_GRET, _ZW_GPOOL, _ZW_GRETBR = 0, 1, 2, 3

HALO = 16
SUPER = 256
VMEM_LIMIT = 56 * 1024 * 1024

BF16 = jnp.bfloat16
F32 = jnp.float32


def _dot(a, b):
    return jnp.dot(a, b, preferred_element_type=F32)


def _rms_scale(h, gain):
    ms = jnp.mean(h * h, axis=-1, keepdims=True)
    return h * lax.rsqrt(ms + RMS_EPS) * gain


def _sigmoid(x):
    return 0.5 * jnp.tanh(0.5 * x) + 0.5


def _prenorm_narrow_kernel(x_ref, g_ref, w_ref, u_ref, zn_ref):
    u = _rms_scale(x_ref[...], g_ref[...]).astype(u_ref.dtype)
    u_ref[...] = u
    zn_ref[...] = _dot(u, w_ref[...].astype(BF16)).astype(zn_ref.dtype)


def _prenorm_narrow(x, gain, w_in, *, tm=512):
    m, d = x.shape
    return pl.pallas_call(
        _prenorm_narrow_kernel,
        out_shape=(jax.ShapeDtypeStruct((m, d), BF16),
                   jax.ShapeDtypeStruct((m, NARROW_WIDTH), BF16)),
        grid=(m // tm,),
        in_specs=[pl.BlockSpec((tm, d), lambda i: (i, 0)),
                  pl.BlockSpec((1, d), lambda i: (0, 0)),
                  pl.BlockSpec((d, NARROW_WIDTH), lambda i: (0, 0), pipeline_mode=pl.Buffered(1))],
        out_specs=(pl.BlockSpec((tm, d), lambda i: (i, 0)),
                   pl.BlockSpec((tm, NARROW_WIDTH), lambda i: (i, 0))),
        compiler_params=pltpu.CompilerParams(
            dimension_semantics=("arbitrary",),
            vmem_limit_bytes=VMEM_LIMIT),
        name="prenorm_narrow",
    )(x, gain, w_in)


CAST_BLOCKS = 32


def _inproj_kernel(u_ref, w_ref, *cast_refs):
    n_cast = (len(cast_refs) - 1) // 2
    z_ref = cast_refs[n_cast]
    z_ref[...] = _dot(u_ref[...], w_ref[...].astype(BF16)).astype(z_ref.dtype)
    for src_ref, dst_ref in zip(cast_refs[:n_cast], cast_refs[n_cast + 1:]):
        dst_ref[...] = src_ref[...].astype(dst_ref.dtype)


def _in_proj(u, w, later_weights, *, tm=2048, tn=1024):
    m, d = u.shape
    skip = NARROW_WIDTH // tn
    nj, ni = w.shape[1] // tn - skip, m // tm
    assert nj * ni >= CAST_BLOCKS
    cast_block = lambda j, i: (jnp.minimum(j * ni + i, CAST_BLOCKS - 1), 0)
    cast_specs = [pl.BlockSpec((cw.shape[0] // CAST_BLOCKS, cw.shape[1]), cast_block)
                  for cw in later_weights]
    outs = pl.pallas_call(
        _inproj_kernel,
        out_shape=[jax.ShapeDtypeStruct((m, nj * tn), BF16)]
                  + [jax.ShapeDtypeStruct(cw.shape, BF16) for cw in later_weights],
        grid=(nj, ni),
        in_specs=[
            pl.BlockSpec((tm, d), lambda j, i: (i, 0)),
            pl.BlockSpec((d, tn), lambda j, i: (0, j + skip)),
        ] + cast_specs,
        out_specs=[pl.BlockSpec((tm, tn), lambda j, i: (i, j))] + cast_specs,
        compiler_params=pltpu.CompilerParams(
            dimension_semantics=("arbitrary", "arbitrary"),
            vmem_limit_bytes=VMEM_LIMIT),
        name="in_proj",
    )(u, w, *later_weights)
    return outs[0], outs[1:]


def _retention_tables():
    half = RET_QK_DIM // 2
    inv = ROPE_BASE ** (-np.arange(half, dtype=np.float64) / half)
    ang = np.arange(SEQ, dtype=np.float64)[:, None] * inv[None, :]
    cos = np.concatenate([np.cos(ang), np.cos(ang)], axis=-1)
    sin = np.concatenate([-np.sin(ang), np.sin(ang)], axis=-1)
    log_g = np.log(1.0 - 2.0 ** (-5.0 - np.arange(RET_HEADS, dtype=np.float64)))
    r = np.arange(SUPER)
    dist = np.abs(r[:, None] - r[None, :]).astype(np.float64)
    visible = (r[None, :] // CHUNK) <= (r[:, None] // CHUNK)
    decay = np.where(visible[None], np.exp(dist[None] * log_g[:, None, None]), 0.0)
    q_decay = np.exp((r + 1.0)[None, :] * log_g[:, None])
    k_decay = np.exp((SUPER - 1.0 - r)[None, :] * log_g[:, None])
    q_decay = np.broadcast_to(q_decay[:, :, None], (RET_HEADS, SUPER, RET_QK_DIM))
    k_decay = np.broadcast_to(k_decay[:, :, None], (RET_HEADS, SUPER, RET_QK_DIM))
    state_decay = [float(v) for v in np.exp(SUPER * log_g)]
    as_f32 = lambda a: jnp.asarray(np.ascontiguousarray(a), dtype=F32)
    return as_f32(cos), as_f32(sin), as_f32(decay), as_f32(q_decay), as_f32(k_decay), state_decay


def _retention_heads(state_decay, q_ref, k_ref, v_ref, g_ref, cos_ref, sin_ref,
                     dec_ref, qd_ref, kd_ref, gain_ref, state_ref):
    cos = cos_ref[...]
    sin = sin_ref[...]
    half = RET_QK_DIM // 2
    heads = []
    for h in range(RET_HEADS):
        qk = slice(h * RET_QK_DIM, (h + 1) * RET_QK_DIM)
        vv = slice(h * RET_V_DIM, (h + 1) * RET_V_DIM)
        q = q_ref[:, qk].astype(F32)
        k = k_ref[:, qk].astype(F32)
        q = q * cos + pltpu.roll(q, half, 1) * sin
        k = (k * cos + pltpu.roll(k, half, 1) * sin) * (RET_QK_DIM ** -0.5)
        v = v_ref[:, vv]
        scores = lax.dot_general(q.astype(BF16), k.astype(BF16),
                                 (((1,), (1,)), ((), ())), preferred_element_type=F32)
        scores = scores * dec_ref[h]
        state = state_ref[h]
        out = _dot(scores.astype(BF16), v) + _dot((q * qd_ref[h]).astype(BF16), state.astype(BF16))
        kv = lax.dot_general((k * kd_ref[h]).astype(BF16), v,
                             (((0,), (0,)), ((), ())), preferred_element_type=F32)
        state_ref[h] = state * state_decay[h] + kv

        mu = jnp.mean(out, axis=-1, keepdims=True)
        dev = out - mu
        var = jnp.mean(dev * dev, axis=-1, keepdims=True)
        normed = dev * lax.rsqrt(var + GN_EPS) * gain_ref[:, vv]
        g = g_ref[:, vv].astype(F32)
        heads.append((g * _sigmoid(g) * normed).astype(BF16))
    return jnp.concatenate(heads, axis=-1)


def _pool_mixed(pos0, p_ref, halo_ref, wg_ref, scale_ref):
    tp = p_ref.shape[0]
    p = p_ref[...]
    halo = halo_ref[...]
    halo = jnp.where(pos0 == 0, jnp.zeros_like(halo), halo)
    r = lax.broadcasted_iota(jnp.int32, (tp, tp), 0)
    c = lax.broadcasted_iota(jnp.int32, (tp, tp), 1)
    rh = lax.broadcasted_iota(jnp.int32, (HALO, HALO), 0)
    ch = lax.broadcasted_iota(jnp.int32, (HALO, HALO), 1) - HALO
    pos = pos0 + lax.broadcasted_iota(jnp.int32, (tp, 1), 0)
    mixed = []
    for gi, w in enumerate(POOL_WINDOWS):
        sl = slice(gi * POOL_GROUP, (gi + 1) * POOL_GROUP)
        pg = p[:, sl]
        a_main = ((c <= r) & (c > r - w)).astype(BF16)
        a_halo = (ch > rh - w).astype(BF16)
        wsum = _dot(a_main, pg)
        wsum = jnp.concatenate([wsum[:HALO] + _dot(a_halo, halo[:, sl]), wsum[HALO:]], axis=0)
        cnt = jnp.minimum(pos + 1, w).astype(F32)
        pooled = wsum / cnt - pg.astype(F32)
        mg = _dot(pooled.astype(BF16), wg_ref[gi]) * scale_ref[:, sl]
        mixed.append(mg.astype(BF16))
    return jnp.concatenate(mixed, axis=-1)


def _mixer_kernel(state_decay, q_ref, k_ref, v_ref, g_ref, gp_ref, gr_ref, p_ref, halo_ref, x_ref,
                  cos_ref, sin_ref, dec_ref, qd_ref, kd_ref, gain_ref,
                  wg_ref, scale_ref, pp_ref, pret_ref, wout_ref, nf_ref,
                  h1_ref, u2_ref, state_ref):
    s = pl.program_id(1)

    @pl.when(s == 0)
    def _():
        state_ref[...] = jnp.zeros_like(state_ref)

    o = _retention_heads(state_decay, q_ref, k_ref, v_ref, g_ref, cos_ref, sin_ref,
                         dec_ref, qd_ref, kd_ref, gain_ref, state_ref)
    y_ret = _dot(o, pret_ref[...])
    mixed = _pool_mixed(s * SUPER, p_ref, halo_ref, wg_ref, scale_ref)
    y_pool = _dot(mixed, pp_ref[...])
    merged = (_sigmoid(gp_ref[...].astype(F32)) * y_pool
              + _sigmoid(gr_ref[...].astype(F32)) * y_ret)
    h1 = x_ref[...] + _dot(merged.astype(BF16), wout_ref[...])
    h1_ref[...] = h1
    u2_ref[...] = _rms_scale(h1, nf_ref[...]).astype(BF16)


def _mixer(zw, zn, x, gn_gain, w_group, pool_scale, w_pool_proj, w_ret, w_out, norm_ffn):
    m = x.shape[0]
    ns = SEQ // SUPER
    halo_per_tile = SUPER // HALO
    cos, sin, decay, q_decay, k_decay, state_decay = _retention_tables()
    row = lambda b, s: b * ns + s
    tile = lambda width, col: pl.BlockSpec((SUPER, width), lambda b, s: (row(b, s), col))
    resident = lambda shape: pl.BlockSpec(shape, lambda b, s: (0,) * len(shape),
                                          pipeline_mode=pl.Buffered(1))
    return pl.pallas_call(
        functools.partial(_mixer_kernel, state_decay),
        out_shape=(jax.ShapeDtypeStruct((m, D_MODEL), F32),
                   jax.ShapeDtypeStruct((m, D_MODEL), BF16)),
        grid=(BATCH, ns),
        in_specs=[
            tile(RET_QK_WIDTH, _ZN_Q),
            tile(RET_QK_WIDTH, _ZN_K),
            tile(RET_V_WIDTH, _ZW_V),
            tile(RET_V_WIDTH, _ZW_GRET),
            tile(D_MODEL, _ZW_GPOOL),
            tile(D_MODEL, _ZW_GRETBR),
            tile(POOL_WIDTH, _ZN_P),
            pl.BlockSpec((HALO, POOL_WIDTH),
                         lambda b, s: (jnp.maximum(row(b, s) * halo_per_tile - 1, 0), _ZN_P)),
            tile(D_MODEL, 0),
            pl.BlockSpec((SUPER, RET_QK_DIM), lambda b, s: (s, 0)),
            pl.BlockSpec((SUPER, RET_QK_DIM), lambda b, s: (s, 0)),
            resident((RET_HEADS, SUPER, SUPER)),
            resident((RET_HEADS, SUPER, RET_QK_DIM)),
            resident((RET_HEADS, SUPER, RET_QK_DIM)),
            resident((1, RET_V_WIDTH)),
            resident((len(POOL_WINDOWS), POOL_GROUP, POOL_GROUP)),
            resident((1, POOL_WIDTH)),
            resident((POOL_WIDTH, D_MODEL)),
            resident((RET_V_WIDTH, D_MODEL)),
            resident((D_MODEL, D_MODEL)),
            resident((1, D_MODEL)),
        ],
        out_specs=(tile(D_MODEL, 0), tile(D_MODEL, 0)),
        scratch_shapes=[pltpu.VMEM((RET_HEADS, RET_QK_DIM, RET_V_DIM), F32)],
        compiler_params=pltpu.CompilerParams(
            dimension_semantics=("arbitrary", "arbitrary"),
            vmem_limit_bytes=VMEM_LIMIT),
        name="mixer",
    )(zn, zn, zw, zw, zw, zw, zn, zn, x, cos, sin, decay, q_decay, k_decay, gn_gain,
      w_group, pool_scale, w_pool_proj, w_ret, w_out, norm_ffn)


CARRY_ROWS = 8


def _up_kernel(u_ref, wg_ref, wu_ref, cw_ref, cb_ref, wd_ref, act_ref, wd_bf16_ref, carry_ref):
    tm = u_ref.shape[0]
    wd_bf16_ref[...] = wd_ref[...].astype(wd_bf16_ref.dtype)

    @pl.when(((pl.program_id(1) * tm) & (SEQ - 1)) == 0)
    def _():
        carry_ref[...] = jnp.zeros_like(carry_ref)

    wg = wg_ref[...].astype(BF16)
    wu = wu_ref[...].astype(BF16)
    carry = carry_ref[...]
    prev1 = carry[CARRY_ROWS - 1:CARRY_ROWS, :]
    prev2 = carry[CARRY_ROWS - 2:CARRY_ROWS - 1, :]
    row = lax.broadcasted_iota(jnp.int32, (UP_SUB_ROWS, act_ref.shape[1]), 0)
    for r in range(tm // UP_SUB_ROWS):
        rows = slice(r * UP_SUB_ROWS, (r + 1) * UP_SUB_ROWS)
        u = u_ref[rows, :]
        gate = _dot(u, wg)
        up = _dot(u, wu)
        g1 = jnp.where(row == 0, prev1, pltpu.roll(gate, 1, 0))
        g2 = jnp.where(row == 0, prev2, jnp.where(row == 1, prev1, pltpu.roll(gate, 2, 0)))
        conv = cw_ref[0:1, :] * g2 + cw_ref[1:2, :] * g1 + cw_ref[2:3, :] * gate + cb_ref[...]
        act_ref[rows, :] = (conv * _sigmoid(conv) * up).astype(act_ref.dtype)
        prev1 = gate[UP_SUB_ROWS - 1:UP_SUB_ROWS, :]
        prev2 = gate[UP_SUB_ROWS - 2:UP_SUB_ROWS - 1, :]
    carry_ref[...] = gate[UP_SUB_ROWS - CARRY_ROWS:, :]


UP_SUB_ROWS = 1024


def _up_glu(u2, w_up, conv_w, conv_b, w_down, *, tm=2048, tn=512):
    m, d = u2.shape
    nj, ni = D_FF // tn, m // tm
    cast_rows = w_down.shape[0] // (nj * ni)
    assert cast_rows * nj * ni == w_down.shape[0]
    cast_spec = pl.BlockSpec((cast_rows, w_down.shape[1]), lambda j, i: (j * ni + i, 0))
    return pl.pallas_call(
        _up_kernel,
        out_shape=(jax.ShapeDtypeStruct((m, D_FF), BF16),
                   jax.ShapeDtypeStruct(w_down.shape, BF16)),
        grid=(nj, ni),
        in_specs=[
            pl.BlockSpec((tm, d), lambda j, i: (i, 0)),
            pl.BlockSpec((d, tn), lambda j, i: (0, j)),
            pl.BlockSpec((d, tn), lambda j, i: (0, nj + j)),
            pl.BlockSpec((3, tn), lambda j, i: (0, j)),
            pl.BlockSpec((1, tn), lambda j, i: (0, j)),
            cast_spec,
        ],
        out_specs=(pl.BlockSpec((tm, tn), lambda j, i: (i, j)), cast_spec),
        scratch_shapes=[pltpu.VMEM((CARRY_ROWS, tn), F32)],
        compiler_params=pltpu.CompilerParams(
            dimension_semantics=("arbitrary", "arbitrary"),
            vmem_limit_bytes=VMEM_LIMIT),
        name="up_glu",
    )(u2, w_up, w_up, conv_w, conv_b, w_down)


def _down_kernel(n_blocks, act0_ref, act1_ref, w0_ref, w1_ref, h1_ref, nf_ref, out_ref, acc_ref):
    k = pl.program_id(1)

    @pl.when(k == 0)
    def _():
        acc_ref[...] = h1_ref[...]

    acc_ref[...] += _dot(act0_ref[...], w0_ref[...])

    @pl.when(2 * k + 1 < n_blocks)
    def _():
        acc_ref[...] += _dot(act1_ref[...], w1_ref[...])

    @pl.when(k == pl.num_programs(1) - 1)
    def _():
        out_ref[...] = _rms_scale(acc_ref[...], nf_ref[...])


def _down_norm(act, w_down, h1, norm_final, *, tm=1024, tk=512):
    m, f = act.shape
    d = w_down.shape[1]
    n_blocks = f // tk
    even = lambda k: 2 * k
    odd = lambda k: jnp.minimum(2 * k + 1, n_blocks - 1)
    return pl.pallas_call(
        functools.partial(_down_kernel, n_blocks),
        out_shape=jax.ShapeDtypeStruct((m, d), F32),
        grid=(m // tm, pl.cdiv(n_blocks, 2)),
        in_specs=[
            pl.BlockSpec((tm, tk), lambda i, k: (i, even(k))),
            pl.BlockSpec((tm, tk), lambda i, k: (i, odd(k))),
            pl.BlockSpec((tk, d), lambda i, k: (even(k), 0)),
            pl.BlockSpec((tk, d), lambda i, k: (odd(k), 0)),
            pl.BlockSpec((tm, d), lambda i, k: (i, 0)),
            pl.BlockSpec((1, d), lambda i, k: (0, 0)),
        ],
        out_specs=pl.BlockSpec((tm, d), lambda i, k: (i, 0)),
        scratch_shapes=[pltpu.VMEM((tm, d), F32)],
        compiler_params=pltpu.CompilerParams(
            dimension_semantics=("arbitrary", "arbitrary"),
            vmem_limit_bytes=VMEM_LIMIT),
        name="down_norm",
    )(act, act, w_down, w_down, h1, norm_final)


def kernel(x, norm_mix, w_in, w_pool_group, pool_scale, w_pool_proj, ret_gn_gain,
           w_ret_proj, w_out, norm_ffn, w_up, conv_w, conv_b, w_down, norm_final):
    assert x.shape == (BATCH, SEQ, D_MODEL) and norm_mix.shape[0] == 1
    h = x.reshape(TOKENS, D_MODEL)
    u, zn = _prenorm_narrow(h, norm_mix, w_in[0])
    zw, (w_pool_proj_b, w_ret_b, w_out_b) = _in_proj(
        u, w_in[0], (w_pool_proj[0], w_ret_proj[0], w_out[0]))
    h1, u2 = _mixer(zw, zn, h, ret_gn_gain, w_pool_group[0].astype(BF16), pool_scale,
                    w_pool_proj_b, w_ret_b, w_out_b, norm_ffn)
    act, w_down_b = _up_glu(u2, w_up[0], conv_w[0], conv_b, w_down[0])
    out = _down_norm(act, w_down_b, h1, norm_final.reshape(1, D_MODEL))
    return out.reshape(BATCH, SEQ, D_MODEL)
```

```python
import functools

import numpy as np
import jax
import jax.numpy as jnp
from jax import lax
from jax.experimental import pallas as pl
from jax.experimental.pallas import tpu as pltpu

D_MODEL = 2048
BATCH = 2
SEQ = 4096
TOKENS = BATCH * SEQ

CHUNK = 64
POOL_WINDOWS = (2, 4, 8, 16)
POOL_WIDTH = D_MODEL // 2
POOL_GROUP = POOL_WIDTH // len(POOL_WINDOWS)
RET_HEADS = 8
RET_QK_DIM = 128
RET_V_DIM = 256
RET_QK_WIDTH = RET_HEADS * RET_QK_DIM
RET_V_WIDTH = RET_HEADS * RET_V_DIM
D_FF = 5632
ROPE_BASE = 10000.0
RMS_EPS = 1e-6
GN_EPS = 1e-5

NARROW_WIDTH = POOL_WIDTH + 2 * RET_QK_WIDTH
_ZN_P, _ZN_Q, _ZN_K = 0, 1, 2
_ZW_V, _ZW_GRET, _ZW_GPOOL, _ZW_GRETBR = 0, 1, 2, 3

HALO = 16
SUPER = 256
VMEM_LIMIT = 56 * 1024 * 1024

BF16 = jnp.bfloat16
F32 = jnp.float32


def _dot(a, b):
    return jnp.dot(a, b, preferred_element_type=F32)


def _rms_scale(h, gain):
    ms = jnp.mean(h * h, axis=-1, keepdims=True)
    return h * lax.rsqrt(ms + RMS_EPS) * gain


def _sigmoid(x):
    return 0.5 * jnp.tanh(0.5 * x) + 0.5


def _prenorm_narrow_kernel(x_ref, g_ref, w_ref, u_ref, zn_ref):
    u = _rms_scale(x_ref[...], g_ref[...]).astype(u_ref.dtype)
    u_ref[...] = u
    zn_ref[...] = _dot(u, w_ref[...].astype(BF16)).astype(zn_ref.dtype)


def _prenorm_narrow(x, gain, w_in, *, tm=512):
    m, d = x.shape
    return pl.pallas_call(
        _prenorm_narrow_kernel,
        out_shape=(jax.ShapeDtypeStruct((m, d), BF16),
                   jax.ShapeDtypeStruct((m, NARROW_WIDTH), BF16)),
        grid=(m // tm,),
        in_specs=[pl.BlockSpec((tm, d), lambda i: (i, 0)),
                  pl.BlockSpec((1, d), lambda i: (0, 0)),
                  pl.BlockSpec((d, NARROW_WIDTH), lambda i: (0, 0), pipeline_mode=pl.Buffered(1))],
        out_specs=(pl.BlockSpec((tm, d), lambda i: (i, 0)),
                   pl.BlockSpec((tm, NARROW_WIDTH), lambda i: (i, 0))),
        compiler_params=pltpu.CompilerParams(
            dimension_semantics=("arbitrary",),
            vmem_limit_bytes=VMEM_LIMIT),
        name="prenorm_narrow",
    )(x, gain, w_in)


CAST_BLOCKS = 32


def _inproj_kernel(u_ref, w_ref, *cast_refs):
    n_cast = (len(cast_refs) - 1) // 2
    z_ref = cast_refs[n_cast]
    z_ref[...] = _dot(u_ref[...], w_ref[...].astype(BF16)).astype(z_ref.dtype)
    for src_ref, dst_ref in zip(cast_refs[:n_cast], cast_refs[n_cast + 1:]):
        dst_ref[...] = src_ref[...].astype(dst_ref.dtype)


def _in_proj(u, w, later_weights, *, tm=2048, tn=1024):
    m, d = u.shape
    skip = NARROW_WIDTH // tn
    nj, ni = w.shape[1] // tn - skip, m // tm
    assert nj * ni >= CAST_BLOCKS
    cast_block = lambda j, i: (jnp.minimum(j * ni + i, CAST_BLOCKS - 1), 0)
    cast_specs = [pl.BlockSpec((cw.shape[0] // CAST_BLOCKS, cw.shape[1]), cast_block)
                  for cw in later_weights]
    outs = pl.pallas_call(
        _inproj_kernel,
        out_shape=[jax.ShapeDtypeStruct((m, nj * tn), BF16)]
                  + [jax.ShapeDtypeStruct(cw.shape, BF16) for cw in later_weights],
        grid=(nj, ni),
        in_specs=[
            pl.BlockSpec((tm, d), lambda j, i: (i, 0)),
            pl.BlockSpec((d, tn), lambda j, i: (0, j + skip)),
        ] + cast_specs,
        out_specs=[pl.BlockSpec((tm, tn), lambda j, i: (i, j))] + cast_specs,
        compiler_params=pltpu.CompilerParams(
            dimension_semantics=("arbitrary", "arbitrary"),
            vmem_limit_bytes=VMEM_LIMIT),
        name="in_proj",
    )(u, w, *later_weights)
    return outs[0], outs[1:]


def _retention_tables():
    half = RET_QK_DIM // 2
    inv = ROPE_BASE ** (-np.arange(half, dtype=np.float64) / half)
    ang = np.arange(SEQ, dtype=np.float64)[:, None] * inv[None, :]
    cos = np.concatenate([np.cos(ang), np.cos(ang)], axis=-1)
    sin = np.concatenate([-np.sin(ang), np.sin(ang)], axis=-1)
    log_g = np.log(1.0 - 2.0 ** (-5.0 - np.arange(RET_HEADS, dtype=np.float64)))
    r = np.arange(SUPER)
    dist = np.abs(r[:, None] - r[None, :]).astype(np.float64)
    visible = (r[None, :] // CHUNK) <= (r[:, None] // CHUNK)
    decay = np.where(visible[None], np.exp(dist[None] * log_g[:, None, None]), 0.0)
    q_decay = np.exp((r + 1.0)[None, :] * log_g[:, None])
    k_decay = np.exp((SUPER - 1.0 - r)[None, :] * log_g[:, None])
    q_decay = np.broadcast_to(q_decay[:, :, None], (RET_HEADS, SUPER, RET_QK_DIM))
    k_decay = np.broadcast_to(k_decay[:, :, None], (RET_HEADS, SUPER, RET_QK_DIM))
    state_decay = [float(v) for v in np.exp(SUPER * log_g)]
    as_f32 = lambda a: jnp.asarray(np.ascontiguousarray(a), dtype=F32)
    return as_f32(cos), as_f32(sin), as_f32(decay), as_f32(q_decay), as_f32(k_decay), state_decay


def _retention_heads(state_decay, q_ref, k_ref, v_ref, g_ref, cos_ref, sin_ref,
                     dec_ref, qd_ref, kd_ref, gain_ref, state_ref):
    cos = cos_ref[...]
    sin = sin_ref[...]
    half = RET_QK_DIM // 2
    heads = []
    for h in range(RET_HEADS):
        qk = slice(h * RET_QK_DIM, (h + 1) * RET_QK_DIM)
        vv = slice(h * RET_V_DIM, (h + 1) * RET_V_DIM)
        q = q_ref[:, qk].astype(F32)
        k = k_ref[:, qk].astype(F32)
        q = q * cos + pltpu.roll(q, half, 1) * sin
        k = (k * cos + pltpu.roll(k, half, 1) * sin) * (RET_QK_DIM ** -0.5)
        v = v_ref[:, vv]
        scores = lax.dot_general(q.astype(BF16), k.astype(BF16),
                                 (((1,), (1,)), ((), ())), preferred_element_type=F32)
        scores = scores * dec_ref[h]
        state = state_ref[h]
        out = _dot(scores.astype(BF16), v) + _dot((q * qd_ref[h]).astype(BF16), state.astype(BF16))
        kv = lax.dot_general((k * kd_ref[h]).astype(BF16), v,
                             (((0,), (0,)), ((), ())), preferred_element_type=F32)
        state_ref[h] = state * state_decay[h] + kv

        mu = jnp.mean(out, axis=-1, keepdims=True)
        dev = out - mu
        var = jnp.mean(dev * dev, axis=-1, keepdims=True)
        normed = dev * lax.rsqrt(var + GN_EPS) * gain_ref[:, vv]
        g = g_ref[:, vv].astype(F32)
        heads.append((g * _sigmoid(g) * normed).astype(BF16))
    return jnp.concatenate(heads, axis=-1)


def _pool_mixed(pos0, p_ref, halo_ref, wg_ref, scale_ref):
    tp = p_ref.shape[0]
    p = p_ref[...]
    halo = halo_ref[...]
    halo = jnp.where(pos0 == 0, jnp.zeros_like(halo), halo)
    r = lax.broadcasted_iota(jnp.int32, (tp, tp), 0)
    c = lax.broadcasted_iota(jnp.int32, (tp, tp), 1)
    rh = lax.broadcasted_iota(jnp.int32, (HALO, HALO), 0)
    ch = lax.broadcasted_iota(jnp.int32, (HALO, HALO), 1) - HALO
    pos = pos0 + lax.broadcasted_iota(jnp.int32, (tp, 1), 0)
    mixed = []
    for gi, w in enumerate(POOL_WINDOWS):
        sl = slice(gi * POOL_GROUP, (gi + 1) * POOL_GROUP)
        pg = p[:, sl]
        a_main = ((c <= r) & (c > r - w)).astype(BF16)
        a_halo = (ch > rh - w).astype(BF16)
        wsum = _dot(a_main, pg)
        wsum = jnp.concatenate([wsum[:HALO] + _dot(a_halo, halo[:, sl]), wsum[HALO:]], axis=0)
        cnt = jnp.minimum(pos + 1, w).astype(F32)
        pooled = wsum / cnt - pg.astype(F32)
        mg = _dot(pooled.astype(BF16), wg_ref[gi]) * scale_ref[:, sl]
        mixed.append(mg.astype(BF16))
    return jnp.concatenate(mixed, axis=-1)


def _mixer_kernel(state_decay, q_ref, k_ref, v_ref, g_ref, gp_ref, gr_ref, p_ref, halo_ref, x_ref,
                  cos_ref, sin_ref, dec_ref, qd_ref, kd_ref, gain_ref,
                  wg_ref, scale_ref, pp_ref, pret_ref, wout_ref, nf_ref,
                  h1_ref, u2_ref, state_ref):
    s = pl.program_id(1)

    @pl.when(s == 0)
    def _():
        state_ref[...] = jnp.zeros_like(state_ref)

    o = _retention_heads(state_decay, q_ref, k_ref, v_ref, g_ref, cos_ref, sin_ref,
                         dec_ref, qd_ref, kd_ref, gain_ref, state_ref)
    y_ret = _dot(o, pret_ref[...])
    mixed = _pool_mixed(s * SUPER, p_ref, halo_ref, wg_ref, scale_ref)
    y_pool = _dot(mixed, pp_ref[...])
    merged = (_sigmoid(gp_ref[...].astype(F32)) * y_pool
              + _sigmoid(gr_ref[...].astype(F32)) * y_ret)
    h1 = x_ref[...] + _dot(merged.astype(BF16), wout_ref[...])
    h1_ref[...] = h1
    u2_ref[...] = _rms_scale(h1, nf_ref[...]).astype(BF16)


def _mixer(zw, zn, x, gn_gain, w_group, pool_scale, w_pool_proj, w_ret, w_out, norm_ffn):
    m = x.shape[0]
    ns = SEQ // SUPER
    halo_per_tile = SUPER // HALO
    cos, sin, decay, q_decay, k_decay, state_decay = _retention_tables()
    row = lambda b, s: b * ns + s
    tile = lambda width, col: pl.BlockSpec((SUPER, width), lambda b, s: (row(b, s), col))
    resident = lambda shape: pl.BlockSpec(shape, lambda b, s: (0,) * len(shape),
                                          pipeline_mode=pl.Buffered(1))
    return pl.pallas_call(
        functools.partial(_mixer_kernel, state_decay),
        out_shape=(jax.ShapeDtypeStruct((m, D_MODEL), F32),
                   jax.ShapeDtypeStruct((m, D_MODEL), BF16)),
        grid=(BATCH, ns),
        in_specs=[
            tile(RET_QK_WIDTH, _ZN_Q),
            tile(RET_QK_WIDTH, _ZN_K),
            tile(RET_V_WIDTH, _ZW_V),
            tile(RET_V_WIDTH, _ZW_GRET),
            tile(D_MODEL, _ZW_GPOOL),
            tile(D_MODEL, _ZW_GRETBR),
            tile(POOL_WIDTH, _ZN_P),
            pl.BlockSpec((HALO, POOL_WIDTH),
                         lambda b, s: (jnp.maximum(row(b, s) * halo_per_tile - 1, 0), _ZN_P)),
            tile(D_MODEL, 0),
            pl.BlockSpec((SUPER, RET_QK_DIM), lambda b, s: (s, 0)),
            pl.BlockSpec((SUPER, RET_QK_DIM), lambda b, s: (s, 0)),
            resident((RET_HEADS, SUPER, SUPER)),
            resident((RET_HEADS, SUPER, RET_QK_DIM)),
            resident((RET_HEADS, SUPER, RET_QK_DIM)),
            resident((1, RET_V_WIDTH)),
            resident((len(POOL_WINDOWS), POOL_GROUP, POOL_GROUP)),
            resident((1, POOL_WIDTH)),
            resident((POOL_WIDTH, D_MODEL)),
            resident((RET_V_WIDTH, D_MODEL)),
            resident((D_MODEL, D_MODEL)),
            resident((1, D_MODEL)),
        ],
        out_specs=(tile(D_MODEL, 0), tile(D_MODEL, 0)),
        scratch_shapes=[pltpu.VMEM((RET_HEADS, RET_QK_DIM, RET_V_DIM), F32)],
        compiler_params=pltpu.CompilerParams(
            dimension_semantics=("arbitrary", "arbitrary"),
            vmem_limit_bytes=VMEM_LIMIT),
        name="mixer",
    )(zn, zn, zw, zw, zw, zw, zn, zn, x, cos, sin, decay, q_decay, k_decay, gn_gain,
      w_group, pool_scale, w_pool_proj, w_ret, w_out, norm_ffn)


CARRY_ROWS = 8


def _up_kernel(u_ref, wg_ref, wu_ref, cw_ref, cb_ref, wd_ref, act_ref, wd_bf16_ref, carry_ref):
    tm = u_ref.shape[0]
    wd_bf16_ref[...] = wd_ref[...].astype(wd_bf16_ref.dtype)

    @pl.when(((pl.program_id(1) * tm) & (SEQ - 1)) == 0)
    def _():
        carry_ref[...] = jnp.zeros_like(carry_ref)

    wg = wg_ref[...].astype(BF16)
    wu = wu_ref[...].astype(BF16)
    carry = carry_ref[...]
    prev1 = carry[CARRY_ROWS - 1:CARRY_ROWS, :]
    prev2 = carry[CARRY_ROWS - 2:CARRY_ROWS - 1, :]
    row = lax.broadcasted_iota(jnp.int32, (UP_SUB_ROWS, act_ref.shape[1]), 0)
    for r in range(tm // UP_SUB_ROWS):
        rows = slice(r * UP_SUB_ROWS, (r + 1) * UP_SUB_ROWS)
        u = u_ref[rows, :]
        gate = _dot(u, wg)
        up = _dot(u, wu)
        g1 = jnp.where(row == 0, prev1, pltpu.roll(gate, 1, 0))
        g2 = jnp.where(row == 0, prev2, jnp.where(row == 1, prev1, pltpu.roll(gate, 2, 0)))
        conv = cw_ref[0:1, :] * g2 + cw_ref[1:2, :] * g1 + cw_ref[2:3, :] * gate + cb_ref[...]
        act_ref[rows, :] = (conv * _sigmoid(conv) * up).astype(act_ref.dtype)
        prev1 = gate[UP_SUB_ROWS - 1:UP_SUB_ROWS, :]
        prev2 = gate[UP_SUB_ROWS - 2:UP_SUB_ROWS - 1, :]
    carry_ref[...] = gate[UP_SUB_ROWS - CARRY_ROWS:, :]


UP_SUB_ROWS = 1024


def _up_glu(u2, w_up, conv_w, conv_b, w_down, *, tm=2048, tn=512):
    m, d = u2.shape
    nj, ni = D_FF // tn, m // tm
    cast_rows = w_down.shape[0] // (nj * ni)
    assert cast_rows * nj * ni == w_down.shape[0]
    cast_spec = pl.BlockSpec((cast_rows, w_down.shape[1]), lambda j, i: (j * ni + i, 0))
    return pl.pallas_call(
        _up_kernel,
        out_shape=(jax.ShapeDtypeStruct((m, D_FF), BF16),
                   jax.ShapeDtypeStruct(w_down.shape, BF16)),
        grid=(nj, ni),
        in_specs=[
            pl.BlockSpec((tm, d), lambda j, i: (i, 0)),
            pl.BlockSpec((d, tn), lambda j, i: (0, j)),
            pl.BlockSpec((d, tn), lambda j, i: (0, nj + j)),
            pl.BlockSpec((3, tn), lambda j, i: (0, j)),
            pl.BlockSpec((1, tn), lambda j, i: (0, j)),
            cast_spec,
        ],
        out_specs=(pl.BlockSpec((tm, tn), lambda j, i: (i, j)), cast_spec),
        scratch_shapes=[pltpu.VMEM((CARRY_ROWS, tn), F32)],
        compiler_params=pltpu.CompilerParams(
            dimension_semantics=("arbitrary", "arbitrary"),
            vmem_limit_bytes=VMEM_LIMIT),
        name="up_glu",
    )(u2, w_up, w_up, conv_w, conv_b, w_down)


def _down_kernel(act_ref, w_ref, h1_ref, nf_ref, out_ref, acc_ref):
    k = pl.program_id(1)

    @pl.when(k == 0)
    def _():
        acc_ref[...] = h1_ref[...]

    acc_ref[...] += _dot(act_ref[...], w_ref[...])

    @pl.when(k == pl.num_programs(1) - 1)
    def _():
        out_ref[...] = _rms_scale(acc_ref[...], nf_ref[...])


def _down_norm(act, w_down, h1, norm_final, *, tm=512, tk=2816):
    m, f = act.shape
    d = w_down.shape[1]
    return pl.pallas_call(
        _down_kernel,
        out_shape=jax.ShapeDtypeStruct((m, d), F32),
        grid=(m // tm, f // tk),
        in_specs=[
            pl.BlockSpec((tm, tk), lambda i, k: (i, k)),
            pl.BlockSpec((tk, d), lambda i, k: (k, 0)),
            pl.BlockSpec((tm, d), lambda i, k: (i, 0)),
            pl.BlockSpec((1, d), lambda i, k: (0, 0)),
        ],
        out_specs=pl.BlockSpec((tm, d), lambda i, k: (i, 0)),
        scratch_shapes=[pltpu.VMEM((tm, d), F32)],
        compiler_params=pltpu.CompilerParams(
            dimension_semantics=("arbitrary", "arbitrary"),
            vmem_limit_bytes=VMEM_LIMIT),
        name="down_norm",
    )(act, w_down, h1, norm_final)


def kernel(x, norm_mix, w_in, w_pool_group, pool_scale, w_pool_proj, ret_gn_gain,
           w_ret_proj, w_out, norm_ffn, w_up, conv_w, conv_b, w_down, norm_final):
    assert x.shape == (BATCH, SEQ, D_MODEL) and norm_mix.shape[0] == 1
    h = x.reshape(TOKENS, D_MODEL)
    u, zn = _prenorm_narrow(h, norm_mix, w_in[0])
    zw, (w_pool_proj_b, w_ret_b, w_out_b) = _in_proj(
        u, w_in[0], (w_pool_proj[0], w_ret_proj[0], w_out[0]))
    h1, u2 = _mixer(zw, zn, h, ret_gn_gain, w_pool_group[0].astype(BF16), pool_scale,
                    w_pool_proj_b, w_ret_b, w_out_b, norm_ffn)
    act, w_down_b = _up_glu(u2, w_up[0], conv_w[0], conv_b, w_down[0])
    out = _down_norm(act, w_down_b, h1, norm_final.reshape(1, D_MODEL))
    return out.reshape(BATCH, SEQ, D_MODEL)
```

```python
import functools

import numpy as np
import jax
import jax.numpy as jnp
from jax import lax
from jax.experimental import pallas as pl
from jax.experimental.pallas import tpu as pltpu

D_MODEL = 2048
BATCH = 2
SEQ = 4096
TOKENS = BATCH * SEQ

CHUNK = 64
POOL_WINDOWS = (2, 4, 8, 16)
POOL_WIDTH = D_MODEL // 2
POOL_GROUP = POOL_WIDTH // len(POOL_WINDOWS)
RET_HEADS = 8
RET_QK_DIM = 128
RET_V_DIM = 256
RET_QK_WIDTH = RET_HEADS * RET_QK_DIM
RET_V_WIDTH = RET_HEADS * RET_V_DIM
D_FF = 5632
ROPE_BASE = 10000.0
RMS_EPS = 1e-6
GN_EPS = 1e-5

NARROW_WIDTH = POOL_WIDTH + 2 * RET_QK_WIDTH
_ZN_P, _ZN_Q, _ZN_K = 0, 1, 2
_ZW_V, _ZW_GRET, _ZW_GPOOL, _ZW_GRETBR = 0, 1, 2, 3

HALO = 16
SUPER = 256
VMEM_LIMIT = 56 * 1024 * 1024

BF16 = jnp.bfloat16
F32 = jnp.float32


def _dot(a, b):
    return jnp.dot(a, b, preferred_element_type=F32)


def _rms_scale(h, gain):
    ms = jnp.mean(h * h, axis=-1, keepdims=True)
    return h * lax.rsqrt(ms + RMS_EPS) * gain


def _sigmoid(x):
    return 0.5 * jnp.tanh(0.5 * x) + 0.5


def _prenorm_narrow_kernel(x_ref, g_ref, w_ref, u_ref, zn_ref):
    u = _rms_scale(x_ref[...], g_ref[...]).astype(u_ref.dtype)
    u_ref[...] = u
    zn_ref[...] = _dot(u, w_ref[...].astype(BF16)).astype(zn_ref.dtype)


def _prenorm_narrow(x, gain, w_in, *, tm=512):
    m, d = x.shape
    return pl.pallas_call(
        _prenorm_narrow_kernel,
        out_shape=(jax.ShapeDtypeStruct((m, d), BF16),
                   jax.ShapeDtypeStruct((m, NARROW_WIDTH), BF16)),
        grid=(m // tm,),
        in_specs=[pl.BlockSpec((tm, d), lambda i: (i, 0)),
                  pl.BlockSpec((1, d), lambda i: (0, 0)),
                  pl.BlockSpec((d, NARROW_WIDTH), lambda i: (0, 0), pipeline_mode=pl.Buffered(1))],
        out_specs=(pl.BlockSpec((tm, d), lambda i: (i, 0)),
                   pl.BlockSpec((tm, NARROW_WIDTH), lambda i: (i, 0))),
        compiler_params=pltpu.CompilerParams(
            dimension_semantics=("arbitrary",),
            vmem_limit_bytes=VMEM_LIMIT),
        name="prenorm_narrow",
    )(x, gain, w_in)


CAST_BLOCKS = 32


def _inproj_kernel(u_ref, w_ref, *cast_refs):
    n_cast = (len(cast_refs) - 1) // 2
    z_ref = cast_refs[n_cast]
    z_ref[...] = _dot(u_ref[...], w_ref[...].astype(BF16)).astype(z_ref.dtype)
    for src_ref, dst_ref in zip(cast_refs[:n_cast], cast_refs[n_cast + 1:]):
        dst_ref[...] = src_ref[...].astype(dst_ref.dtype)


def _in_proj(u, w, later_weights, *, tm=2048, tn=1024):
    m, d = u.shape
    skip = NARROW_WIDTH // tn
    nj, ni = w.shape[1] // tn - skip, m // tm
    assert nj * ni >= CAST_BLOCKS
    cast_block = lambda j, i: (jnp.minimum(j * ni + i, CAST_BLOCKS - 1), 0)
    cast_specs = [pl.BlockSpec((cw.shape[0] // CAST_BLOCKS, cw.shape[1]), cast_block)
                  for cw in later_weights]
    outs = pl.pallas_call(
        _inproj_kernel,
        out_shape=[jax.ShapeDtypeStruct((m, nj * tn), BF16)]
                  + [jax.ShapeDtypeStruct(cw.shape, BF16) for cw in later_weights],
        grid=(nj, ni),
        in_specs=[
            pl.BlockSpec((tm, d), lambda j, i: (i, 0)),
            pl.BlockSpec((d, tn), lambda j, i: (0, j + skip)),
        ] + cast_specs,
        out_specs=[pl.BlockSpec((tm, tn), lambda j, i: (i, j))] + cast_specs,
        compiler_params=pltpu.CompilerParams(
            dimension_semantics=("arbitrary", "arbitrary"),
            vmem_limit_bytes=VMEM_LIMIT),
        name="in_proj",
    )(u, w, *later_weights)
    return outs[0], outs[1:]


def _retention_tables():
    half = RET_QK_DIM // 2
    inv = ROPE_BASE ** (-np.arange(half, dtype=np.float64) / half)
    ang = np.arange(SEQ, dtype=np.float64)[:, None] * inv[None, :]
    cos = np.concatenate([np.cos(ang), np.cos(ang)], axis=-1)
    sin = np.concatenate([-np.sin(ang), np.sin(ang)], axis=-1)
    log_g = np.log(1.0 - 2.0 ** (-5.0 - np.arange(RET_HEADS, dtype=np.float64)))
    r = np.arange(SUPER)
    dist = np.abs(r[:, None] - r[None, :]).astype(np.float64)
    visible = (r[None, :] // CHUNK) <= (r[:, None] // CHUNK)
    decay = np.where(visible[None], np.exp(dist[None] * log_g[:, None, None]), 0.0)
    q_decay = np.exp((r + 1.0)[None, :] * log_g[:, None])
    k_decay = np.exp((SUPER - 1.0 - r)[None, :] * log_g[:, None])
    q_decay = np.broadcast_to(q_decay[:, :, None], (RET_HEADS, SUPER, RET_QK_DIM))
    k_decay = np.broadcast_to(k_decay[:, :, None], (RET_HEADS, SUPER, RET_QK_DIM))
    state_decay = [float(v) for v in np.exp(SUPER * log_g)]
    as_f32 = lambda a: jnp.asarray(np.ascontiguousarray(a), dtype=F32)
    return as_f32(cos), as_f32(sin), as_f32(decay), as_f32(q_decay), as_f32(k_decay), state_decay


def _retention_heads(state_decay, q_ref, k_ref, v_ref, g_ref, cos_ref, sin_ref,
                     dec_ref, qd_ref, kd_ref, gain_ref, state_ref):
    cos = cos_ref[...]
    sin = sin_ref[...]
    half = RET_QK_DIM // 2
    heads = []
    for h in range(RET_HEADS):
        qk = slice(h * RET_QK_DIM, (h + 1) * RET_QK_DIM)
        vv = slice(h * RET_V_DIM, (h + 1) * RET_V_DIM)
        q = q_ref[:, qk].astype(F32)
        k = k_ref[:, qk].astype(F32)
        q = q * cos + pltpu.roll(q, half, 1) * sin
        k = (k * cos + pltpu.roll(k, half, 1) * sin) * (RET_QK_DIM ** -0.5)
        v = v_ref[:, vv]
        scores = lax.dot_general(q.astype(BF16), k.astype(BF16),
                                 (((1,), (1,)), ((), ())), preferred_element_type=F32)
        scores = scores * dec_ref[h]
        state = state_ref[h]
        out = _dot(scores.astype(BF16), v) + _dot((q * qd_ref[h]).astype(BF16), state.astype(BF16))
        kv = lax.dot_general((k * kd_ref[h]).astype(BF16), v,
                             (((0,), (0,)), ((), ())), preferred_element_type=F32)
        state_ref[h] = state * state_decay[h] + kv

        mu = jnp.mean(out, axis=-1, keepdims=True)
        dev = out - mu
        var = jnp.mean(dev * dev, axis=-1, keepdims=True)
        normed = dev * lax.rsqrt(var + GN_EPS) * gain_ref[:, vv]
        g = g_ref[:, vv].astype(F32)
        heads.append((g * _sigmoid(g) * normed).astype(BF16))
    return jnp.concatenate(heads, axis=-1)


def _pool_mixed(pos0, p_ref, halo_ref, wg_ref, scale_ref):
    tp = p_ref.shape[0]
    p = p_ref[...]
    halo = halo_ref[...]
    halo = jnp.where(pos0 == 0, jnp.zeros_like(halo), halo)
    r = lax.broadcasted_iota(jnp.int32, (tp, tp), 0)
    c = lax.broadcasted_iota(jnp.int32, (tp, tp), 1)
    rh = lax.broadcasted_iota(jnp.int32, (HALO, HALO), 0)
    ch = lax.broadcasted_iota(jnp.int32, (HALO, HALO), 1) - HALO
    pos = pos0 + lax.broadcasted_iota(jnp.int32, (tp, 1), 0)
    mixed = []
    for gi, w in enumerate(POOL_WINDOWS):
        sl = slice(gi * POOL_GROUP, (gi + 1) * POOL_GROUP)
        pg = p[:, sl]
        a_main = ((c <= r) & (c > r - w)).astype(BF16)
        a_halo = (ch > rh - w).astype(BF16)
        wsum = _dot(a_main, pg)
        wsum = jnp.concatenate([wsum[:HALO] + _dot(a_halo, halo[:, sl]), wsum[HALO:]], axis=0)
        cnt = jnp.minimum(pos + 1, w).astype(F32)
        pooled = wsum / cnt - pg.astype(F32)
        mg = _dot(pooled.astype(BF16), wg_ref[gi]) * scale_ref[:, sl]
        mixed.append(mg.astype(BF16))
    return jnp.concatenate(mixed, axis=-1)


def _mixer_kernel(state_decay, q_ref, k_ref, v_ref, g_ref, gp_ref, gr_ref, p_ref, halo_ref, x_ref,
                  cos_ref, sin_ref, dec_ref, qd_ref, kd_ref, gain_ref,
                  wg_ref, scale_ref, pp_ref, pret_ref, wout_ref, nf_ref,
                  h1_ref, u2_ref, state_ref):
    s = pl.program_id(1)

    @pl.when(s == 0)
    def _():
        state_ref[...] = jnp.zeros_like(state_ref)

    o = _retention_heads(state_decay, q_ref, k_ref, v_ref, g_ref, cos_ref, sin_ref,
                         dec_ref, qd_ref, kd_ref, gain_ref, state_ref)
    y_ret = _dot(o, pret_ref[...])
    mixed = _pool_mixed(s * SUPER, p_ref, halo_ref, wg_ref, scale_ref)
    y_pool = _dot(mixed, pp_ref[...])
    merged = (_sigmoid(gp_ref[...].astype(F32)) * y_pool
              + _sigmoid(gr_ref[...].astype(F32)) * y_ret)
    h1 = x_ref[...] + _dot(merged.astype(BF16), wout_ref[...])
    h1_ref[...] = h1
    u2_ref[...] = _rms_scale(h1, nf_ref[...]).astype(BF16)


def _mixer(zw, zn, x, gn_gain, w_group, pool_scale, w_pool_proj, w_ret, w_out, norm_ffn):
    m = x.shape[0]
    ns = SEQ // SUPER
    halo_per_tile = SUPER // HALO
    cos, sin, decay, q_decay, k_decay, state_decay = _retention_tables()
    row = lambda b, s: b * ns + s
    tile = lambda width, col: pl.BlockSpec((SUPER, width), lambda b, s: (row(b, s), col))
    resident = lambda shape: pl.BlockSpec(shape, lambda b, s: (0,) * len(shape),
                                          pipeline_mode=pl.Buffered(1))
    return pl.pallas_call(
        functools.partial(_mixer_kernel, state_decay),
        out_shape=(jax.ShapeDtypeStruct((m, D_MODEL), F32),
                   jax.ShapeDtypeStruct((m, D_MODEL), BF16)),
        grid=(BATCH, ns),
        in_specs=[
            tile(RET_QK_WIDTH, _ZN_Q),
            tile(RET_QK_WIDTH, _ZN_K),
            tile(RET_V_WIDTH, _ZW_V),
            tile(RET_V_WIDTH, _ZW_GRET),
            tile(D_MODEL, _ZW_GPOOL),
            tile(D_MODEL, _ZW_GRETBR),
            tile(POOL_WIDTH, _ZN_P),
            pl.BlockSpec((HALO, POOL_WIDTH),
                         lambda b, s: (jnp.maximum(row(b, s) * halo_per_tile - 1, 0), _ZN_P)),
            tile(D_MODEL, 0),
            pl.BlockSpec((SUPER, RET_QK_DIM), lambda b, s: (s, 0)),
            pl.BlockSpec((SUPER, RET_QK_DIM), lambda b, s: (s, 0)),
            resident((RET_HEADS, SUPER, SUPER)),
            resident((RET_HEADS, SUPER, RET_QK_DIM)),
            resident((RET_HEADS, SUPER, RET_QK_DIM)),
            resident((1, RET_V_WIDTH)),
            resident((len(POOL_WINDOWS), POOL_GROUP, POOL_GROUP)),
            resident((1, POOL_WIDTH)),
            resident((POOL_WIDTH, D_MODEL)),
            resident((RET_V_WIDTH, D_MODEL)),
            resident((D_MODEL, D_MODEL)),
            resident((1, D_MODEL)),
        ],
        out_specs=(tile(D_MODEL, 0), tile(D_MODEL, 0)),
        scratch_shapes=[pltpu.VMEM((RET_HEADS, RET_QK_DIM, RET_V_DIM), F32)],
        compiler_params=pltpu.CompilerParams(
            dimension_semantics=("arbitrary", "arbitrary"),
            vmem_limit_bytes=VMEM_LIMIT),
        name="mixer",
    )(zn, zn, zw, zw, zw, zw, zn, zn, x, cos, sin, decay, q_decay, k_decay, gn_gain,
      w_group, pool_scale, w_pool_proj, w_ret, w_out, norm_ffn)


CARRY_ROWS = 8


def _up_kernel(u_ref, wg_ref, wu_ref, cw_ref, cb_ref, wd_ref, act_ref, wd_bf16_ref, carry_ref):
    tm = u_ref.shape[0]
    wd_bf16_ref[...] = wd_ref[...].astype(wd_bf16_ref.dtype)

    @pl.when(((pl.program_id(1) * tm) & (SEQ - 1)) == 0)
    def _():
        carry_ref[...] = jnp.zeros_like(carry_ref)

    wg = wg_ref[...].astype(BF16)
    wu = wu_ref[...].astype(BF16)
    carry = carry_ref[...]
    prev1 = carry[CARRY_ROWS - 1:CARRY_ROWS, :]
    prev2 = carry[CARRY_ROWS - 2:CARRY_ROWS - 1, :]
    row = lax.broadcasted_iota(jnp.int32, (UP_SUB_ROWS, act_ref.shape[1]), 0)
    for r in range(tm // UP_SUB_ROWS):
        rows = slice(r * UP_SUB_ROWS, (r + 1) * UP_SUB_ROWS)
        u = u_ref[rows, :]
        gate = _dot(u, wg)
        up = _dot(u, wu)
        g1 = jnp.where(row == 0, prev1, pltpu.roll(gate, 1, 0))
        g2 = jnp.where(row == 0, prev2, jnp.where(row == 1, prev1, pltpu.roll(gate, 2, 0)))
        conv = cw_ref[0:1, :] * g2 + cw_ref[1:2, :] * g1 + cw_ref[2:3, :] * gate + cb_ref[...]
        act_ref[rows, :] = (conv * _sigmoid(conv) * up).astype(act_ref.dtype)
        prev1 = gate[UP_SUB_ROWS - 1:UP_SUB_ROWS, :]
        prev2 = gate[UP_SUB_ROWS - 2:UP_SUB_ROWS - 1, :]
    carry_ref[...] = gate[UP_SUB_ROWS - CARRY_ROWS:, :]


UP_SUB_ROWS = 1024


def _up_glu(u2, w_up, conv_w, conv_b, w_down, *, tm=2048, tn=512):
    m, d = u2.shape
    nj, ni = D_FF // tn, m // tm
    cast_rows = w_down.shape[0] // (nj * ni)
    assert cast_rows * nj * ni == w_down.shape[0]
    cast_spec = pl.BlockSpec((cast_rows, w_down.shape[1]), lambda j, i: (j * ni + i, 0))
    return pl.pallas_call(
        _up_kernel,
        out_shape=(jax.ShapeDtypeStruct((m, D_FF), BF16),
                   jax.ShapeDtypeStruct(w_down.shape, BF16)),
        grid=(nj, ni),
        in_specs=[
            pl.BlockSpec((tm, d), lambda j, i: (i, 0)),
            pl.BlockSpec((d, tn), lambda j, i: (0, j)),
            pl.BlockSpec((d, tn), lambda j, i: (0, nj + j)),
            pl.BlockSpec((3, tn), lambda j, i: (0, j)),
            pl.BlockSpec((1, tn), lambda j, i: (0, j)),
            cast_spec,
        ],
        out_specs=(pl.BlockSpec((tm, tn), lambda j, i: (i, j)), cast_spec),
        scratch_shapes=[pltpu.VMEM((CARRY_ROWS, tn), F32)],
        compiler_params=pltpu.CompilerParams(
            dimension_semantics=("arbitrary", "arbitrary"),
            vmem_limit_bytes=VMEM_LIMIT),
        name="up_glu",
    )(u2, w_up, w_up, conv_w, conv_b, w_down)


def _down_kernel(act_ref, w_ref, h1_ref, nf_ref, out_ref):
    out_ref[...] = h1_ref[...] + _dot(act_ref[...], w_ref[...])
    out_ref[...] = _rms_scale(out_ref[...], nf_ref[...])


def _down_norm(act, w_down, h1, norm_final, *, tm=512):
    m, f = act.shape
    d = w_down.shape[1]
    return pl.pallas_call(
        _down_kernel,
        out_shape=jax.ShapeDtypeStruct((m, d), F32),
        grid=(m // tm,),
        in_specs=[
            pl.BlockSpec((tm, f), lambda i: (i, 0)),
            pl.BlockSpec((f, d), lambda i: (0, 0), pipeline_mode=pl.Buffered(1)),
            pl.BlockSpec((tm, d), lambda i: (i, 0)),
            pl.BlockSpec((1, d), lambda i: (0, 0)),
        ],
        out_specs=pl.BlockSpec((tm, d), lambda i: (i, 0)),
        compiler_params=pltpu.CompilerParams(
            dimension_semantics=("arbitrary",),
            vmem_limit_bytes=VMEM_LIMIT),
        name="down_norm",
    )(act, w_down, h1, norm_final)


def kernel(x, norm_mix, w_in, w_pool_group, pool_scale, w_pool_proj, ret_gn_gain,
           w_ret_proj, w_out, norm_ffn, w_up, conv_w, conv_b, w_down, norm_final):
    assert x.shape == (BATCH, SEQ, D_MODEL) and norm_mix.shape[0] == 1
    h = x.reshape(TOKENS, D_MODEL)
    u, zn = _prenorm_narrow(h, norm_mix, w_in[0])
    zw, (w_pool_proj_b, w_ret_b, w_out_b) = _in_proj(
        u, w_in[0], (w_pool_proj[0], w_ret_proj[0], w_out[0]))
    h1, u2 = _mixer(zw, zn, h, ret_gn_gain, w_pool_group[0].astype(BF16), pool_scale,
                    w_pool_proj_b, w_ret_b, w_out_b, norm_ffn)
    act, w_down_b = _up_glu(u2, w_up[0], conv_w[0], conv_b, w_down[0])
    out = _down_norm(act, w_down_b, h1, norm_final.reshape(1, D_MODEL))
    return out.reshape(BATCH, SEQ, D_MODEL)
```

```python
import functools

import numpy as np
import jax
import jax.numpy as jnp
from jax import lax
from jax.experimental import pallas as pl
from jax.experimental.pallas import tpu as pltpu

D_MODEL = 2048
BATCH = 2
SEQ = 4096
TOKENS = BATCH * SEQ

CHUNK = 64
POOL_WINDOWS = (2, 4, 8, 16)
POOL_WIDTH = D_MODEL // 2
POOL_GROUP = POOL_WIDTH // len(POOL_WINDOWS)
RET_HEADS = 8
RET_QK_DIM = 128
RET_V_DIM = 256
RET_QK_WIDTH = RET_HEADS * RET_QK_DIM
RET_V_WIDTH = RET_HEADS * RET_V_DIM
D_FF = 5632
ROPE_BASE = 10000.0
RMS_EPS = 1e-6
GN_EPS = 1e-5

NARROW_WIDTH = POOL_WIDTH + 2 * RET_QK_WIDTH
_ZN_P, _ZN_Q, _ZN_K = 0, 1, 2
_ZW_V, _ZW_GRET, _ZW_GPOOL, _ZW_GRETBR = 0, 1, 2, 3

HALO = 16
SUPER = 256
VMEM_LIMIT = 56 * 1024 * 1024

BF16 = jnp.bfloat16
F32 = jnp.float32


def _dot(a, b):
    return jnp.dot(a, b, preferred_element_type=F32)


def _rms_scale(h, gain):
    ms = jnp.mean(h * h, axis=-1, keepdims=True)
    return h * lax.rsqrt(ms + RMS_EPS) * gain


def _sigmoid(x):
    return 0.5 * jnp.tanh(0.5 * x) + 0.5


def _prenorm_narrow_kernel(x_ref, g_ref, w_ref, u_ref, zn_ref):
    u = _rms_scale(x_ref[...], g_ref[...]).astype(u_ref.dtype)
    u_ref[...] = u
    zn_ref[...] = _dot(u, w_ref[...].astype(BF16)).astype(zn_ref.dtype)


def _prenorm_narrow(x, gain, w_in, *, tm=512):
    m, d = x.shape
    return pl.pallas_call(
        _prenorm_narrow_kernel,
        out_shape=(jax.ShapeDtypeStruct((m, d), BF16),
                   jax.ShapeDtypeStruct((m, NARROW_WIDTH), BF16)),
        grid=(m // tm,),
        in_specs=[pl.BlockSpec((tm, d), lambda i: (i, 0)),
                  pl.BlockSpec((1, d), lambda i: (0, 0)),
                  pl.BlockSpec((d, NARROW_WIDTH), lambda i: (0, 0), pipeline_mode=pl.Buffered(1))],
        out_specs=(pl.BlockSpec((tm, d), lambda i: (i, 0)),
                   pl.BlockSpec((tm, NARROW_WIDTH), lambda i: (i, 0))),
        compiler_params=pltpu.CompilerParams(
            dimension_semantics=("arbitrary",),
            vmem_limit_bytes=VMEM_LIMIT),
        name="prenorm_narrow",
    )(x, gain, w_in)


CAST_BLOCKS = 32


def _inproj_kernel(u_ref, w_ref, *cast_refs):
    n_cast = (len(cast_refs) - 1) // 2
    z_ref = cast_refs[n_cast]
    z_ref[...] = _dot(u_ref[...], w_ref[...].astype(BF16)).astype(z_ref.dtype)
    for src_ref, dst_ref in zip(cast_refs[:n_cast], cast_refs[n_cast + 1:]):
        dst_ref[...] = src_ref[...].astype(dst_ref.dtype)


def _in_proj(u, w, later_weights, *, tm=2048, tn=1024):
    m, d = u.shape
    skip = NARROW_WIDTH // tn
    nj, ni = w.shape[1] // tn - skip, m // tm
    assert nj * ni >= CAST_BLOCKS
    cast_block = lambda j, i: (jnp.minimum(j * ni + i, CAST_BLOCKS - 1), 0)
    cast_specs = [pl.BlockSpec((cw.shape[0] // CAST_BLOCKS, cw.shape[1]), cast_block)
                  for cw in later_weights]
    outs = pl.pallas_call(
        _inproj_kernel,
        out_shape=[jax.ShapeDtypeStruct((m, nj * tn), BF16)]
                  + [jax.ShapeDtypeStruct(cw.shape, BF16) for cw in later_weights],
        grid=(nj, ni),
        in_specs=[
            pl.BlockSpec((tm, d), lambda j, i: (i, 0)),
            pl.BlockSpec((d, tn), lambda j, i: (0, j + skip)),
        ] + cast_specs,
        out_specs=[pl.BlockSpec((tm, tn), lambda j, i: (i, j))] + cast_specs,
        compiler_params=pltpu.CompilerParams(
            dimension_semantics=("arbitrary", "arbitrary"),
            vmem_limit_bytes=VMEM_LIMIT),
        name="in_proj",
    )(u, w, *later_weights)
    return outs[0], outs[1:]


def _retention_tables():
    half = RET_QK_DIM // 2
    inv = ROPE_BASE ** (-np.arange(half, dtype=np.float64) / half)
    ang = np.arange(SEQ, dtype=np.float64)[:, None] * inv[None, :]
    cos = np.concatenate([np.cos(ang), np.cos(ang)], axis=-1)
    sin = np.concatenate([-np.sin(ang), np.sin(ang)], axis=-1)
    log_g = np.log(1.0 - 2.0 ** (-5.0 - np.arange(RET_HEADS, dtype=np.float64)))
    r = np.arange(SUPER)
    dist = np.abs(r[:, None] - r[None, :]).astype(np.float64)
    visible = (r[None, :] // CHUNK) <= (r[:, None] // CHUNK)
    decay = np.where(visible[None], np.exp(dist[None] * log_g[:, None, None]), 0.0)
    q_decay = np.exp((r + 1.0)[None, :] * log_g[:, None])
    k_decay = np.exp((SUPER - 1.0 - r)[None, :] * log_g[:, None])
    q_decay = np.broadcast_to(q_decay[:, :, None], (RET_HEADS, SUPER, RET_QK_DIM))
    k_decay = np.broadcast_to(k_decay[:, :, None], (RET_HEADS, SUPER, RET_QK_DIM))
    state_decay = [float(v) for v in np.exp(SUPER * log_g)]
    as_f32 = lambda a: jnp.asarray(np.ascontiguousarray(a), dtype=F32)
    return as_f32(cos), as_f32(sin), as_f32(decay), as_f32(q_decay), as_f32(k_decay), state_decay


def _retention_heads(state_decay, q_ref, k_ref, v_ref, g_ref, cos_ref, sin_ref,
                     dec_ref, qd_ref, kd_ref, gain_ref, state_ref):
    cos = cos_ref[...]
    sin = sin_ref[...]
    half = RET_QK_DIM // 2
    heads = []
    for h in range(RET_HEADS):
        qk = slice(h * RET_QK_DIM, (h + 1) * RET_QK_DIM)
        vv = slice(h * RET_V_DIM, (h + 1) * RET_V_DIM)
        q = q_ref[:, qk].astype(F32)
        k = k_ref[:, qk].astype(F32)
        q = q * cos + pltpu.roll(q, half, 1) * sin
        k = (k * cos + pltpu.roll(k, half, 1) * sin) * (RET_QK_DIM ** -0.5)
        v = v_ref[:, vv]
        scores = lax.dot_general(q.astype(BF16), k.astype(BF16),
                                 (((1,), (1,)), ((), ())), preferred_element_type=F32)
        scores = scores * dec_ref[h]
        state = state_ref[h]
        out = _dot(scores.astype(BF16), v) + _dot((q * qd_ref[h]).astype(BF16), state.astype(BF16))
        kv = lax.dot_general((k * kd_ref[h]).astype(BF16), v,
                             (((0,), (0,)), ((), ())), preferred_element_type=F32)
        state_ref[h] = state * state_decay[h] + kv

        mu = jnp.mean(out, axis=-1, keepdims=True)
        dev = out - mu
        var = jnp.mean(dev * dev, axis=-1, keepdims=True)
        normed = dev * lax.rsqrt(var + GN_EPS) * gain_ref[:, vv]
        g = g_ref[:, vv].astype(F32)
        heads.append((g * _sigmoid(g) * normed).astype(BF16))
    return jnp.concatenate(heads, axis=-1)


def _pool_mixed(pos0, p_ref, halo_ref, wg_ref, scale_ref):
    tp = p_ref.shape[0]
    p = p_ref[...]
    halo = halo_ref[...]
    halo = jnp.where(pos0 == 0, jnp.zeros_like(halo), halo)
    r = lax.broadcasted_iota(jnp.int32, (tp, tp), 0)
    c = lax.broadcasted_iota(jnp.int32, (tp, tp), 1)
    rh = lax.broadcasted_iota(jnp.int32, (HALO, HALO), 0)
    ch = lax.broadcasted_iota(jnp.int32, (HALO, HALO), 1) - HALO
    pos = pos0 + lax.broadcasted_iota(jnp.int32, (tp, 1), 0)
    mixed = []
    for gi, w in enumerate(POOL_WINDOWS):
        sl = slice(gi * POOL_GROUP, (gi + 1) * POOL_GROUP)
        pg = p[:, sl]
        a_main = ((c <= r) & (c > r - w)).astype(BF16)
        a_halo = (ch > rh - w).astype(BF16)
        wsum = _dot(a_main, pg)
        wsum = jnp.concatenate([wsum[:HALO] + _dot(a_halo, halo[:, sl]), wsum[HALO:]], axis=0)
        cnt = jnp.minimum(pos + 1, w).astype(F32)
        pooled = wsum / cnt - pg.astype(F32)
        mg = _dot(pooled.astype(BF16), wg_ref[gi]) * scale_ref[:, sl]
        mixed.append(mg.astype(BF16))
    return jnp.concatenate(mixed, axis=-1)


def _mixer_kernel(state_decay, q_ref, k_ref, v_ref, g_ref, gp_ref, gr_ref, p_ref, halo_ref, x_ref,
                  cos_ref, sin_ref, dec_ref, qd_ref, kd_ref, gain_ref,
                  wg_ref, scale_ref, pp_ref, pret_ref, wout_ref, nf_ref,
                  h1_ref, u2_ref, state_ref, o_ref, mixed_ref):
    t = pl.program_id(0)
    tiles_per_seq = SEQ // SUPER
    seq_tile = jnp.minimum(t, pl.num_programs(0) - 2) & (tiles_per_seq - 1)

    @pl.when(t == 0)
    def _():
        o_ref[...] = jnp.zeros_like(o_ref)
        mixed_ref[...] = jnp.zeros_like(mixed_ref)

    @pl.when(seq_tile == 0)
    def _():
        state_ref[...] = jnp.zeros_like(state_ref)

    prev = (t + 1) & 1
    y_ret = _dot(o_ref[prev], pret_ref[...])
    y_pool = _dot(mixed_ref[prev], pp_ref[...])
    merged = (_sigmoid(gp_ref[...].astype(F32)) * y_pool
              + _sigmoid(gr_ref[...].astype(F32)) * y_ret)
    h1 = x_ref[...] + _dot(merged.astype(BF16), wout_ref[...])
    h1_ref[...] = h1
    u2_ref[...] = _rms_scale(h1, nf_ref[...]).astype(BF16)

    cur = t & 1
    o_ref[cur] = _retention_heads(state_decay, q_ref, k_ref, v_ref, g_ref, cos_ref, sin_ref,
                                  dec_ref, qd_ref, kd_ref, gain_ref, state_ref)
    mixed_ref[cur] = _pool_mixed(seq_tile * SUPER, p_ref, halo_ref, wg_ref, scale_ref)


def _mixer(zw, zn, x, gn_gain, w_group, pool_scale, w_pool_proj, w_ret, w_out, norm_ffn):
    m = x.shape[0]
    n_tiles = m // SUPER
    tiles_per_seq = SEQ // SUPER
    halo_per_tile = SUPER // HALO
    cos, sin, decay, q_decay, k_decay, state_decay = _retention_tables()
    branch_row = lambda t: jnp.minimum(t, n_tiles - 1)
    proj_row = lambda t: jnp.maximum(t - 1, 0)
    branch = lambda width, col: pl.BlockSpec((SUPER, width), lambda t: (branch_row(t), col))
    tile = lambda width, col: pl.BlockSpec((SUPER, width), lambda t: (proj_row(t), col))
    rope = pl.BlockSpec((SUPER, RET_QK_DIM), lambda t: (branch_row(t) & (tiles_per_seq - 1), 0))
    resident = lambda shape: pl.BlockSpec(shape, lambda t: (0,) * len(shape),
                                          pipeline_mode=pl.Buffered(1))
    return pl.pallas_call(
        functools.partial(_mixer_kernel, state_decay),
        out_shape=(jax.ShapeDtypeStruct((m, D_MODEL), F32),
                   jax.ShapeDtypeStruct((m, D_MODEL), BF16)),
        grid=(n_tiles + 1,),
        in_specs=[
            branch(RET_QK_WIDTH, _ZN_Q),
            branch(RET_QK_WIDTH, _ZN_K),
            branch(RET_V_WIDTH, _ZW_V),
            branch(RET_V_WIDTH, _ZW_GRET),
            tile(D_MODEL, _ZW_GPOOL),
            tile(D_MODEL, _ZW_GRETBR),
            branch(POOL_WIDTH, _ZN_P),
            pl.BlockSpec((HALO, POOL_WIDTH),
                         lambda t: (jnp.maximum(branch_row(t) * halo_per_tile - 1, 0), _ZN_P)),
            tile(D_MODEL, 0),
            rope,
            rope,
            resident((RET_HEADS, SUPER, SUPER)),
            resident((RET_HEADS, SUPER, RET_QK_DIM)),
            resident((RET_HEADS, SUPER, RET_QK_DIM)),
            resident((1, RET_V_WIDTH)),
            resident((len(POOL_WINDOWS), POOL_GROUP, POOL_GROUP)),
            resident((1, POOL_WIDTH)),
            resident((POOL_WIDTH, D_MODEL)),
            resident((RET_V_WIDTH, D_MODEL)),
            resident((D_MODEL, D_MODEL)),
            resident((1, D_MODEL)),
        ],
        out_specs=(tile(D_MODEL, 0), tile(D_MODEL, 0)),
        scratch_shapes=[pltpu.VMEM((RET_HEADS, RET_QK_DIM, RET_V_DIM), F32),
                        pltpu.VMEM((2, SUPER, RET_V_WIDTH), BF16),
                        pltpu.VMEM((2, SUPER, POOL_WIDTH), BF16)],
        compiler_params=pltpu.CompilerParams(
            dimension_semantics=("arbitrary",),
            vmem_limit_bytes=VMEM_LIMIT),
        name="mixer",
    )(zn, zn, zw, zw, zw, zw, zn, zn, x, cos, sin, decay, q_decay, k_decay, gn_gain,
      w_group, pool_scale, w_pool_proj, w_ret, w_out, norm_ffn)


CARRY_ROWS = 8


def _up_kernel(u_ref, wg_ref, wu_ref, cw_ref, cb_ref, wd_ref, act_ref, wd_bf16_ref, carry_ref):
    tm = u_ref.shape[0]
    wd_bf16_ref[...] = wd_ref[...].astype(wd_bf16_ref.dtype)

    @pl.when(((pl.program_id(1) * tm) & (SEQ - 1)) == 0)
    def _():
        carry_ref[...] = jnp.zeros_like(carry_ref)

    wg = wg_ref[...].astype(BF16)
    wu = wu_ref[...].astype(BF16)
    carry = carry_ref[...]
    prev1 = carry[CARRY_ROWS - 1:CARRY_ROWS, :]
    prev2 = carry[CARRY_ROWS - 2:CARRY_ROWS - 1, :]
    row = lax.broadcasted_iota(jnp.int32, (UP_SUB_ROWS, act_ref.shape[1]), 0)
    for r in range(tm // UP_SUB_ROWS):
        rows = slice(r * UP_SUB_ROWS, (r + 1) * UP_SUB_ROWS)
        u = u_ref[rows, :]
        gate = _dot(u, wg)
        up = _dot(u, wu)
        g1 = jnp.where(row == 0, prev1, pltpu.roll(gate, 1, 0))
        g2 = jnp.where(row == 0, prev2, jnp.where(row == 1, prev1, pltpu.roll(gate, 2, 0)))
        conv = cw_ref[0:1, :] * g2 + cw_ref[1:2, :] * g1 + cw_ref[2:3, :] * gate + cb_ref[...]
        act_ref[rows, :] = (conv * _sigmoid(conv) * up).astype(act_ref.dtype)
        prev1 = gate[UP_SUB_ROWS - 1:UP_SUB_ROWS, :]
        prev2 = gate[UP_SUB_ROWS - 2:UP_SUB_ROWS - 1, :]
    carry_ref[...] = gate[UP_SUB_ROWS - CARRY_ROWS:, :]


UP_SUB_ROWS = 1024


def _up_glu(u2, w_up, conv_w, conv_b, w_down, *, tm=2048, tn=512):
    m, d = u2.shape
    nj, ni = D_FF // tn, m // tm
    cast_rows = w_down.shape[0] // (nj * ni)
    assert cast_rows * nj * ni == w_down.shape[0]
    cast_spec = pl.BlockSpec((cast_rows, w_down.shape[1]), lambda j, i: (j * ni + i, 0))
    return pl.pallas_call(
        _up_kernel,
        out_shape=(jax.ShapeDtypeStruct((m, D_FF), BF16),
                   jax.ShapeDtypeStruct(w_down.shape, BF16)),
        grid=(nj, ni),
        in_specs=[
            pl.BlockSpec((tm, d), lambda j, i: (i, 0)),
            pl.BlockSpec((d, tn), lambda j, i: (0, j)),
            pl.BlockSpec((d, tn), lambda j, i: (0, nj + j)),
            pl.BlockSpec((3, tn), lambda j, i: (0, j)),
            pl.BlockSpec((1, tn), lambda j, i: (0, j)),
            cast_spec,
        ],
        out_specs=(pl.BlockSpec((tm, tn), lambda j, i: (i, j)), cast_spec),
        scratch_shapes=[pltpu.VMEM((CARRY_ROWS, tn), F32)],
        compiler_params=pltpu.CompilerParams(
            dimension_semantics=("arbitrary", "arbitrary"),
            vmem_limit_bytes=VMEM_LIMIT),
        name="up_glu",
    )(u2, w_up, w_up, conv_w, conv_b, w_down)


def _down_kernel(act_ref, w_ref, h1_ref, nf_ref, out_ref):
    out_ref[...] = h1_ref[...] + _dot(act_ref[...], w_ref[...])
    out_ref[...] = _rms_scale(out_ref[...], nf_ref[...])


def _down_norm(act, w_down, h1, norm_final, *, tm=512):
    m, f = act.shape
    d = w_down.shape[1]
    return pl.pallas_call(
        _down_kernel,
        out_shape=jax.ShapeDtypeStruct((m, d), F32),
        grid=(m // tm,),
        in_specs=[
            pl.BlockSpec((tm, f), lambda i: (i, 0)),
            pl.BlockSpec((f, d), lambda i: (0, 0), pipeline_mode=pl.Buffered(1)),
            pl.BlockSpec((tm, d), lambda i: (i, 0)),
            pl.BlockSpec((1, d), lambda i: (0, 0)),
        ],
        out_specs=pl.BlockSpec((tm, d), lambda i: (i, 0)),
        compiler_params=pltpu.CompilerParams(
            dimension_semantics=("arbitrary",),
            vmem_limit_bytes=VMEM_LIMIT),
        name="down_norm",
    )(act, w_down, h1, norm_final)


def kernel(x, norm_mix, w_in, w_pool_group, pool_scale, w_pool_proj, ret_gn_gain,
           w_ret_proj, w_out, norm_ffn, w_up, conv_w, conv_b, w_down, norm_final):
    assert x.shape == (BATCH, SEQ, D_MODEL) and norm_mix.shape[0] == 1
    h = x.reshape(TOKENS, D_MODEL)
    u, zn = _prenorm_narrow(h, norm_mix, w_in[0])
    zw, (w_pool_proj_b, w_ret_b, w_out_b) = _in_proj(
        u, w_in[0], (w_pool_proj[0], w_ret_proj[0], w_out[0]))
    h1, u2 = _mixer(zw, zn, h, ret_gn_gain, w_pool_group[0].astype(BF16), pool_scale,
                    w_pool_proj_b, w_ret_b, w_out_b, norm_ffn)
    act, w_down_b = _up_glu(u2, w_up[0], conv_w[0], conv_b, w_down[0])
    out = _down_norm(act, w_down_b, h1, norm_final.reshape(1, D_MODEL))
    return out.reshape(BATCH, SEQ, D_MODEL)
```
